```python
import jax, jax.numpy as jnp
from jax import lax
import numpy as np

D_MODEL = 1024
BATCH = 16
SEQ = 256
DEPTH = 4
DEC_BATCH = 4
DEC_SEQ = 4096
PAST_LEN = 256

GRID_W = 64
N_MIXERS = 2
N_MLA = (DEPTH + 1) // 2
N_CONV = DEPTH // 2
N_HEADS = 8
Q_LORA = 512
KV_LORA = 256
QK_NOPE = 128
QK_ROPE = 64
QK_DIM = QK_NOPE + QK_ROPE
V_DIM = 128
ROPE_THETA = 10000.0
CONV_WIDTH = 31
CONV_PAD = (CONV_WIDTH - 1) // 2
D_FF = ((8 * D_MODEL // 3 + 255) // 256) * 256
Q_BLOCK = 128
EPS = 1e-6

kernel_name = "hybrid_mla_conformer_diffusion_step"


def rms_norm(x, g):
    xf = x.astype(jnp.float32)
    y = xf * lax.rsqrt(jnp.mean(xf * xf, axis=-1, keepdims=True) + EPS)
    return (y * g.astype(jnp.float32)).astype(x.dtype)


def layer_norm(x, g, b):
    xf = x.astype(jnp.float32)
    mu = jnp.mean(xf, axis=-1, keepdims=True)
    var = jnp.mean(jnp.square(xf - mu), axis=-1, keepdims=True)
    y = (xf - mu) * lax.rsqrt(var + EPS)
    return (y * g.astype(jnp.float32) + b.astype(jnp.float32)).astype(x.dtype)


def modulation(cond, w_mod, b_mod):
    m = jax.nn.silu(cond) @ w_mod + b_mod
    return jnp.split(m[..., None, :], 6, axis=-1)


def axial_rope_tables(T):
    rows = T // GRID_W
    t = jnp.arange(T)
    row = jnp.repeat(jnp.arange(rows), GRID_W).astype(jnp.float32)
    col = (t % GRID_W).astype(jnp.float32)
    ax = QK_ROPE // 2
    inv = 1.0 / (ROPE_THETA ** (jnp.arange(0, ax, 2, dtype=jnp.float32) / ax))
    ar = row[:, None] * inv[None, :]
    ac = col[:, None] * inv[None, :]
    cos = jnp.concatenate([jnp.cos(ar), jnp.cos(ar), jnp.cos(ac), jnp.cos(ac)], axis=-1)
    sin = jnp.concatenate([jnp.sin(ar), jnp.sin(ar), jnp.sin(ac), jnp.sin(ac)], axis=-1)
    return cos, sin


def apply_rope(x, cos, sin):
    nope, pe = x[..., :QK_NOPE], x[..., QK_NOPE:].astype(jnp.float32)
    r1, r2, c1, c2 = jnp.split(pe, 4, axis=-1)
    rot = jnp.concatenate([-r2, r1, -c2, c1], axis=-1)
    pe = pe * cos[None, :, None, :] + rot * sin[None, :, None, :]
    return jnp.concatenate([nope, pe.astype(x.dtype)], axis=-1)


def mla_compress(h, w_dq, q_norm, w_uq, w_dkv, kv_norm, q_hn):
    B, T, _ = h.shape
    cq = rms_norm(h @ w_dq, q_norm)
    q = rms_norm((cq @ w_uq).reshape(B, T, N_HEADS, QK_DIM), q_hn)
    kv = h @ w_dkv
    ckv = rms_norm(kv[..., :KV_LORA], kv_norm)
    kpe = kv[..., KV_LORA:]
    return q, ckv, kpe


def mla_expand(ckv, kpe, w_ukv, k_hn):
    B, T, _ = ckv.shape
    kvu = (ckv @ w_ukv).reshape(B, T, N_HEADS, QK_NOPE + V_DIM)
    k_nope, v = kvu[..., :QK_NOPE], kvu[..., QK_NOPE:]
    k_pe = jnp.broadcast_to(kpe[:, :, None, :], (B, T, N_HEADS, QK_ROPE))
    k = rms_norm(jnp.concatenate([k_nope, k_pe], axis=-1), k_hn)
    return k, v


def block_attention(q, k, v):
    B, T, H, Dk = q.shape
    nb = T // Q_BLOCK
    qb = q.reshape(B, nb, Q_BLOCK, H, Dk).transpose(1, 0, 2, 3, 4)
    scale = Dk ** -0.5

    def one_block(qblk):
        s = jnp.einsum('bqhd,bkhd->bhqk', qblk, k).astype(jnp.float32) * scale
        p = jax.nn.softmax(s, axis=-1).astype(v.dtype)
        return jnp.einsum('bhqk,bkhd->bqhd', p, v)

    o = lax.map(one_block, qb)
    return o.transpose(1, 0, 2, 3, 4).reshape(B, T, H * V_DIM)


def conformer_conv(h, w_pw1, b_pw1, w_dw, b_dw, ln_g, ln_b, w_pw2, b_pw2):
    a = h @ w_pw1 + b_pw1
    g = a[..., :D_MODEL] * jax.nn.sigmoid(a[..., D_MODEL:])
    y = lax.conv_general_dilated(
        g, w_dw[:, None, :].astype(g.dtype), window_strides=(1,),
        padding=[(CONV_PAD, CONV_PAD)], dimension_numbers=('NWC', 'WIO', 'NWC'),
        feature_group_count=D_MODEL) + b_dw
    y = jax.nn.silu(layer_norm(y, ln_g, ln_b))
    return y @ w_pw2 + b_pw2


def swiglu(h, w_gate, w_up, w_down):
    return (jax.nn.silu(h @ w_gate) * (h @ w_up)) @ w_down


def setup_inputs(seed: int = 0) -> dict:
    key = jax.random.key(seed)
    ks = iter(jax.random.split(key, 64))
    f32 = jnp.float32

    def w(shape, fan_in, mult=1.0):
        return jax.random.normal(next(ks), shape, f32) * (mult * fan_in ** -0.5)

    def gain(shape):
        return 1.0 + 0.02 * jax.random.normal(next(ks), shape, f32)

    def bias(shape):
        return 0.01 * jax.random.normal(next(ks), shape, f32)

    D = D_MODEL
    return {
        "x_prompt": jax.random.normal(next(ks), (BATCH, SEQ, D), f32),
        "x_sample": jax.random.normal(next(ks), (DEC_BATCH, DEC_SEQ, D), f32),
        "cache_ckv": jax.random.normal(next(ks), (DEC_BATCH, N_MLA, PAST_LEN, KV_LORA), f32),
        "cache_kpe": jax.random.normal(next(ks), (DEC_BATCH, N_MLA, PAST_LEN, QK_ROPE), f32),
        "c": jax.random.normal(next(ks), (DEC_BATCH, D), f32),
        "c_ctx": jax.random.normal(next(ks), (D,), f32),
        "w_mod": w((DEPTH, D, 6 * D), D, 0.5),
        "b_mod": bias((DEPTH, 6 * D)),
        "norm_mix": gain((DEPTH, D)),
        "norm_ffn": gain((DEPTH, D)),
        "mla_w_dq": w((N_MLA, D, Q_LORA), D),
        "mla_q_norm": gain((N_MLA, Q_LORA)),
        "mla_w_uq": w((N_MLA, Q_LORA, N_HEADS * QK_DIM), Q_LORA),
        "mla_w_dkv": w((N_MLA, D, KV_LORA + QK_ROPE), D),
        "mla_kv_norm": gain((N_MLA, KV_LORA)),
        "mla_w_ukv": w((N_MLA, KV_LORA, N_HEADS * (QK_NOPE + V_DIM)), KV_LORA),
        "mla_q_hnorm": gain((N_MLA, QK_DIM)),
        "mla_k_hnorm": gain((N_MLA, QK_DIM)),
        "mla_w_o": w((N_MLA, N_HEADS * V_DIM, D), N_HEADS * V_DIM),
        "conv_w_pw1": w((N_CONV, D, 2 * D), D),
        "conv_b_pw1": bias((N_CONV, 2 * D)),
        "conv_w_dw": w((N_CONV, CONV_WIDTH, D), CONV_WIDTH),
        "conv_b_dw": bias((N_CONV, D)),
        "conv_ln_g": gain((N_CONV, D)),
        "conv_ln_b": bias((N_CONV, D)),
        "conv_w_pw2": w((N_CONV, D, D), D),
        "conv_b_pw2": bias((N_CONV, D)),
        "ffn_w_gate": w((DEPTH, D, D_FF), D),
        "ffn_w_up": w((DEPTH, D, D_FF), D),
        "ffn_w_down": w((DEPTH, D_FF, D), D_FF),
    }


def reference(x_prompt, x_sample, cache_ckv, cache_kpe, c, c_ctx,
              w_mod, b_mod, norm_mix, norm_ffn,
              mla_w_dq, mla_q_norm, mla_w_uq, mla_w_dkv, mla_kv_norm, mla_w_ukv,
              mla_q_hnorm, mla_k_hnorm, mla_w_o,
              conv_w_pw1, conv_b_pw1, conv_w_dw, conv_b_dw, conv_ln_g, conv_ln_b,
              conv_w_pw2, conv_b_pw2,
              ffn_w_gate, ffn_w_up, ffn_w_down):
    xp = x_prompt
    xs = x_sample
    cos, sin = axial_rope_tables(xs.shape[1])
    new_ckv, new_kpe = [], []

    for l in range(DEPTH):
        sh1_c, sc1_c, g1_c, sh2_c, sc2_c, g2_c = modulation(c_ctx, w_mod[l], b_mod[l])
        sh1_s, sc1_s, g1_s, sh2_s, sc2_s, g2_s = modulation(c, w_mod[l], b_mod[l])
        hp = rms_norm(xp, norm_mix[l]) * (1.0 + sc1_c) + sh1_c
        hs = rms_norm(xs, norm_mix[l]) * (1.0 + sc1_s) + sh1_s
        j = l // N_MIXERS

        if l % N_MIXERS == 0:
            args = (mla_w_dq[j], mla_q_norm[j], mla_w_uq[j], mla_w_dkv[j],
                    mla_kv_norm[j], mla_q_hnorm[j])
            qp, ckv_p, kpe_p = mla_compress(hp, *args)
            kp, vp = mla_expand(ckv_p, kpe_p, mla_w_ukv[j], mla_k_hnorm[j])
            op = block_attention(qp, kp, vp) @ mla_w_o[j]
            new_ckv.append(ckv_p)
            new_kpe.append(kpe_p)
            qs, ckv_s, kpe_s = mla_compress(hs, *args)
            qs = apply_rope(qs, cos, sin)
            ks_, vs_ = mla_expand(ckv_s, kpe_s, mla_w_ukv[j], mla_k_hnorm[j])
            ks_ = apply_rope(ks_, cos, sin)
            kc, vc = mla_expand(cache_ckv[:, j], cache_kpe[:, j], mla_w_ukv[j], mla_k_hnorm[j])
            os_ = block_attention(qs, jnp.concatenate([kc, ks_], axis=1),
                                  jnp.concatenate([vc, vs_], axis=1)) @ mla_w_o[j]
        else:
            cargs = (conv_w_pw1[j], conv_b_pw1[j], conv_w_dw[j], conv_b_dw[j],
                     conv_ln_g[j], conv_ln_b[j], conv_w_pw2[j], conv_b_pw2[j])
            op = conformer_conv(hp, *cargs)
            os_ = conformer_conv(hs, *cargs)

        xp = xp + g1_c * op
        xs = xs + g1_s * os_

        hp = rms_norm(xp, norm_ffn[l]) * (1.0 + sc2_c) + sh2_c
        hs = rms_norm(xs, norm_ffn[l]) * (1.0 + sc2_s) + sh2_s
        xp = xp + g2_c * swiglu(hp, ffn_w_gate[l], ffn_w_up[l], ffn_w_down[l])
        xs = xs + g2_s * swiglu(hs, ffn_w_gate[l], ffn_w_up[l], ffn_w_down[l])

    state_ckv = jnp.stack(new_ckv, axis=1)
    state_kpe = jnp.stack(new_kpe, axis=1)
    return (xp, xs, state_ckv, state_kpe)
```

```python
import functools

import jax
import jax.numpy as jnp
from jax import lax
from jax.experimental import pallas as pl
from jax.experimental.pallas import tpu as pltpu

F32 = jnp.float32
BF16 = jnp.bfloat16

D_MODEL = 1024
DEPTH = 4
N_MOD = 6
GRID_W = 64
N_HEADS = 8
Q_LORA = 512
KV_LORA = 256
QK_NOPE = 128
QK_ROPE = 64
QK_DIM = QK_NOPE + QK_ROPE
V_DIM = 128
HEAD_PAD = 256
PE_PAD = HEAD_PAD - QK_NOPE
ROPE_THETA = 10000.0
CONV_WIDTH = 31
CONV_PAD = (CONV_WIDTH - 1) // 2
CONV_HALO = 16
D_FF = 2816
FF_CHUNK = 1408
EPS = 1e-6
COND_ROWS = 8

VMEM_LIMIT = 56 * 1024 * 1024


def _dot(a, b):
    return jnp.dot(a, b, preferred_element_type=F32)


def _silu(a):
    return a / (1.0 + jnp.exp(-a))


def _rms(x, n):
    return x * lax.rsqrt(jnp.sum(x * x, axis=-1, keepdims=True) * (1.0 / n) + EPS)


def _mod_norm(x, gain, scale, shift):
    return (_rms(x, x.shape[-1]) * gain) * (1.0 + scale) + shift


def _split_bf16(a):
    hi = a.astype(BF16)
    lo = (a - hi.astype(F32)).astype(BF16)
    return hi, lo


def _mod_kernel(c_ref, w_ref, b_ref, o_ref):
    s_hi, s_lo = _split_bf16(_silu(c_ref[...]))
    w_hi, w_lo = _split_bf16(w_ref[0])
    o_ref[0] = _dot(s_hi, w_hi) + _dot(s_lo, w_hi) + _dot(s_hi, w_lo) + b_ref[0]


def _modulation(cond, w_mod, b_mod):
    tn = 1536
    n_out = N_MOD * D_MODEL
    return pl.pallas_call(
        _mod_kernel,
        grid=(DEPTH, n_out // tn),
        in_specs=[
            pl.BlockSpec((COND_ROWS, D_MODEL), lambda l, n: (0, 0)),
            pl.BlockSpec((1, D_MODEL, tn), lambda l, n: (l, 0, n)),
            pl.BlockSpec((1, 1, tn), lambda l, n: (l, 0, n)),
        ],
        out_specs=pl.BlockSpec((1, COND_ROWS, tn), lambda l, n: (l, 0, n)),
        out_shape=jax.ShapeDtypeStruct((DEPTH, COND_ROWS, n_out), F32),
        compiler_params=pltpu.CompilerParams(
            dimension_semantics=("arbitrary", "arbitrary"), vmem_limit_bytes=VMEM_LIMIT),
        name="modulation",
    )(cond, w_mod, b_mod.reshape(DEPTH, 1, n_out))


def _expand_kv(ckv, kpe_rot, kpe_ss, wukv_ref, khn_ref, k_ref, v_ref):
    kvu = _dot(ckv.astype(BF16), wukv_ref[...])
    gain_nope = khn_ref[:, :QK_NOPE]
    for h in range(N_HEADS):
        kn = kvu[:, h * 256:h * 256 + QK_NOPE]
        ss = jnp.sum(kn * kn, axis=-1, keepdims=True) + kpe_ss
        inv = lax.rsqrt(ss * (1.0 / QK_DIM) + EPS)
        k_ref[0, h, :, :QK_NOPE] = (kn * inv * gain_nope).astype(BF16)
        k_ref[0, h, :, QK_NOPE:] = (kpe_rot * inv).astype(BF16)
        v_ref[0, h] = kvu[:, h * 256 + QK_NOPE:(h + 1) * 256].astype(BF16)


def _rope(pe, cos, sin_a, sin_b):
    return (pe * cos + pltpu.roll(pe, PE_PAD - QK_ROPE // 4, axis=1) * sin_a
            + pltpu.roll(pe, QK_ROPE // 4, axis=1) * sin_b)


def _mla_proj_kernel(use_rope, emit_state, *refs):
    refs = list(refs)
    x_ref, mod_ref, gmix_ref, wdq_ref, qn_ref, wuq_ref, wdkv_ref, kvn_ref, wukv_ref, qhn_ref, khn_ref = refs[:11]
    refs = refs[11:]
    if use_rope:
        cos_ref, sina_ref, sinb_ref = refs[:3]
        refs = refs[5:]
    q_ref, k_ref, v_ref = refs[:3]
    refs = refs[3:]

    m = mod_ref[0]
    h = _mod_norm(x_ref[0], gmix_ref[...], m[1:2], m[0:1]).astype(BF16)

    kv = _dot(h, wdkv_ref[...])
    ckv = _rms(kv[:, :KV_LORA], KV_LORA) * kvn_ref[...]
    kpe = kv[:, KV_LORA:]
    if emit_state:
        ckv_ref, kpe_ref = refs
        ckv_ref[0] = ckv
        kpe_ref[0] = kpe[:, :QK_ROPE]
    kpe_ss = jnp.sum(kpe * kpe, axis=-1, keepdims=True)
    kpe_rot = kpe * khn_ref[:, QK_NOPE:]
    if use_rope:
        cos, sin_a, sin_b = cos_ref[...], sina_ref[...], sinb_ref[...]
        kpe_rot = _rope(kpe_rot, cos, sin_a, sin_b)
    _expand_kv(ckv, kpe_rot, kpe_ss, wukv_ref, khn_ref, k_ref, v_ref)

    cq = (_rms(_dot(h, wdq_ref[...]), Q_LORA) * qn_ref[...]).astype(BF16)
    qf = _dot(cq, wuq_ref[...])
    scale = QK_DIM ** -0.5
    for hd in range(N_HEADS):
        qh = qf[:, hd * HEAD_PAD:(hd + 1) * HEAD_PAD]
        inv = lax.rsqrt(jnp.sum(qh * qh, axis=-1, keepdims=True) * (1.0 / QK_DIM) + EPS) * scale
        qn = qh * inv * qhn_ref[...]
        q_ref[0, hd, :, :QK_NOPE] = qn[:, :QK_NOPE].astype(BF16)
        qpe = qn[:, QK_NOPE:]
        if use_rope:
            qpe = _rope(qpe, cos, sin_a, sin_b)
        q_ref[0, hd, :, QK_NOPE:] = qpe.astype(BF16)


def _cache_expand_kernel(ckv_ref, kpe_ref, wukv_ref, khn_ref, k_ref, v_ref):
    kpe = kpe_ref[0]
    kpe_ss = jnp.sum(kpe * kpe, axis=-1, keepdims=True)
    _expand_kv(ckv_ref[0], kpe * khn_ref[:, QK_NOPE:], kpe_ss, wukv_ref, khn_ref, k_ref, v_ref)


def _const_spec(shape):
    nd = len(shape)
    return pl.BlockSpec(shape, lambda *_: (0,) * nd)


def _mla_weight_specs(w):
    return [_const_spec(w[n].shape) for n in
            ("wdq", "qn", "wuq", "wdkv", "kvn", "wukv", "qhn", "khn")]


def _mla_weight_args(w):
    return [w[n] for n in ("wdq", "qn", "wuq", "wdkv", "kvn", "wukv", "qhn", "khn")]


def _mla_proj_prompt(x, mod, gmix, w):
    b, t, _ = x.shape
    tm = t
    kern = functools.partial(_mla_proj_kernel, False, True)
    return pl.pallas_call(
        kern,
        grid=(b, t // tm),
        in_specs=[
            pl.BlockSpec((1, tm, D_MODEL), lambda i, j: (i, j, 0)),
            pl.BlockSpec((1, N_MOD, D_MODEL), lambda i, j: (0, 0, 0)),
            _const_spec((1, D_MODEL)),
        ] + _mla_weight_specs(w),
        out_specs=[
            pl.BlockSpec((1, N_HEADS, tm, HEAD_PAD), lambda i, j: (i, 0, j, 0)),
            pl.BlockSpec((1, N_HEADS, tm, HEAD_PAD), lambda i, j: (i, 0, j, 0)),
            pl.BlockSpec((1, N_HEADS, tm, V_DIM), lambda i, j: (i, 0, j, 0)),
            pl.BlockSpec((1, tm, KV_LORA), lambda i, j: (i, j, 0)),
            pl.BlockSpec((1, tm, QK_ROPE), lambda i, j: (i, j, 0)),
        ],
        out_shape=[
            jax.ShapeDtypeStruct((b, N_HEADS, t, HEAD_PAD), BF16),
            jax.ShapeDtypeStruct((b, N_HEADS, t, HEAD_PAD), BF16),
            jax.ShapeDtypeStruct((b, N_HEADS, t, V_DIM), BF16),
            jax.ShapeDtypeStruct((b, t, KV_LORA), F32),
            jax.ShapeDtypeStruct((b, t, QK_ROPE), F32),
        ],
        compiler_params=pltpu.CompilerParams(
            dimension_semantics=("arbitrary", "arbitrary"), vmem_limit_bytes=VMEM_LIMIT),
        name="mla_proj_prompt",
    )(x, mod, gmix, *_mla_weight_args(w))


def _mla_proj_sample(x, mod, gmix, w, rope, k_all, v_all):
    b, t, _ = x.shape
    tm = 512
    kern = functools.partial(_mla_proj_kernel, True, False)
    rope_spec = pl.BlockSpec((tm, PE_PAD), lambda i, j: (j, 0))
    return pl.pallas_call(
        kern,
        grid=(b, t // tm),
        in_specs=[
            pl.BlockSpec((1, tm, D_MODEL), lambda i, j: (i, j, 0)),
            pl.BlockSpec((1, N_MOD, D_MODEL), lambda i, j: (i + 1, 0, 0)),
            _const_spec((1, D_MODEL)),
        ] + _mla_weight_specs(w) + [
            rope_spec, rope_spec, rope_spec,
            pl.BlockSpec(memory_space=pl.ANY),
            pl.BlockSpec(memory_space=pl.ANY),
        ],
        out_specs=[
            pl.BlockSpec((1, N_HEADS, tm, HEAD_PAD), lambda i, j: (i, 0, j, 0)),
            pl.BlockSpec((1, N_HEADS, tm, HEAD_PAD), lambda i, j: (i, 0, j, 0)),
            pl.BlockSpec((1, N_HEADS, tm, V_DIM), lambda i, j: (i, 0, j, 0)),
        ],
        out_shape=[
            jax.ShapeDtypeStruct((b, N_HEADS, t, HEAD_PAD), BF16),
            jax.ShapeDtypeStruct(k_all.shape, BF16),
            jax.ShapeDtypeStruct(v_all.shape, BF16),
        ],
        input_output_aliases={14: 1, 15: 2},
        compiler_params=pltpu.CompilerParams(
            dimension_semantics=("arbitrary", "arbitrary"), vmem_limit_bytes=VMEM_LIMIT),
        name="mla_proj_sample",
    )(x, mod, gmix, *_mla_weight_args(w), *rope, k_all, v_all)


def _cache_expand(cache_ckv, cache_kpe_pad, j, w, t_latent):
    b, _, past, _ = cache_ckv.shape
    blk = t_latent // past
    t_all = t_latent + past
    return pl.pallas_call(
        _cache_expand_kernel,
        grid=(b,),
        in_specs=[
            pl.BlockSpec((1, None, past, KV_LORA), lambda i: (i, j, 0, 0)),
            pl.BlockSpec((1, None, past, PE_PAD), lambda i: (i, j, 0, 0)),
            _const_spec(w["wukv"].shape),
            _const_spec(w["khn"].shape),
        ],
        out_specs=[
            pl.BlockSpec((1, N_HEADS, past, HEAD_PAD), lambda i: (i, 0, blk, 0)),
            pl.BlockSpec((1, N_HEADS, past, V_DIM), lambda i: (i, 0, blk, 0)),
        ],
        out_shape=[
            jax.ShapeDtypeStruct((b, N_HEADS, t_all, HEAD_PAD), BF16),
            jax.ShapeDtypeStruct((b, N_HEADS, t_all, V_DIM), BF16),
        ],
        compiler_params=pltpu.CompilerParams(
            dimension_semantics=("arbitrary",), vmem_limit_bytes=VMEM_LIMIT),
        name="cache_expand",
    )(cache_ckv, cache_kpe_pad, w["wukv"], w["khn"])


def _attn_kernel(q_ref, k_ref, v_ref, o_ref):
    s = lax.dot_general(q_ref[0, 0], k_ref[0, 0], (((1,), (1,)), ((), ())),
                        preferred_element_type=F32)
    p = jnp.exp(s - jnp.max(s, axis=-1, keepdims=True))
    l = jnp.sum(p, axis=-1, keepdims=True)
    o = _dot(p.astype(BF16), v_ref[0, 0])
    o_ref[0] = (o / l).astype(BF16)


def _attention(q, k, v, tq):
    b, _, t, _ = q.shape
    tk = k.shape[2]
    return pl.pallas_call(
        _attn_kernel,
        grid=(b, N_HEADS, t // tq),
        in_specs=[
            pl.BlockSpec((1, 1, tq, HEAD_PAD), lambda i, h, j: (i, h, j, 0)),
            pl.BlockSpec((1, 1, tk, HEAD_PAD), lambda i, h, j: (i, h, 0, 0)),
            pl.BlockSpec((1, 1, tk, V_DIM), lambda i, h, j: (i, h, 0, 0)),
        ],
        out_specs=pl.BlockSpec((1, tq, V_DIM), lambda i, h, j: (i, j, h)),
        out_shape=jax.ShapeDtypeStruct((b, t, N_HEADS * V_DIM), BF16),
        compiler_params=pltpu.CompilerParams(
            dimension_semantics=("arbitrary", "arbitrary", "arbitrary"),
            vmem_limit_bytes=VMEM_LIMIT),
        name="attention",
    )(q, k, v)


def _conv_kernel(tm, xp_ref, x_ref, xn_ref, mod_ref, gmix_ref, w1_ref, b1_ref, wdw_ref, bdw_ref,
                 lng_ref, lnb_ref, w2_ref, b2_ref, o_ref, g_ref, y_ref):
    j = pl.program_id(1)
    rows = tm + 2 * CONV_HALO
    m = mod_ref[0]
    x = x_ref[0]
    x_ext = jnp.concatenate([xp_ref[0], x, xn_ref[0]], axis=0)
    h = _mod_norm(x_ext, gmix_ref[...], m[1:2], m[0:1]).astype(BF16)
    a = _dot(h, w1_ref[...]) + b1_ref[...]
    g = a[:, :D_MODEL] / (1.0 + jnp.exp(-a[:, D_MODEL:]))
    r = lax.broadcasted_iota(jnp.int32, (rows, 1), 0)
    valid = jnp.logical_and(jnp.logical_or(r >= CONV_HALO, j > 0),
                            jnp.logical_or(r < tm + CONV_HALO, j < pl.num_programs(1) - 1))
    g_ref[...] = jnp.where(valid, g, 0.0)

    rc = 128
    first = CONV_HALO - CONV_PAD
    for r0 in range(0, tm, rc):
        for c in range(D_MODEL // 128):
            cs = slice(c * 128, (c + 1) * 128)
            acc = jnp.zeros((rc, 128), F32)
            for k in range(CONV_WIDTH):
                acc = acc + g_ref[pl.ds(r0 + first + k, rc), cs] * wdw_ref[k:k + 1, cs]
            y_ref[r0:r0 + rc, cs] = acc + bdw_ref[:, cs]

    y = y_ref[...]
    mu = jnp.mean(y, axis=-1, keepdims=True)
    yc = y - mu
    var = jnp.mean(yc * yc, axis=-1, keepdims=True)
    z = _silu(yc * lax.rsqrt(var + EPS) * lng_ref[...] + lnb_ref[...]).astype(BF16)
    o_ref[0] = x + m[2:3] * (_dot(z, w2_ref[...]) + b2_ref[...])


def _conv_module(x, mod, cond_of_batch, gmix, w, tm):
    b, t, _ = x.shape
    nh = tm // CONV_HALO
    last = t // CONV_HALO - 1
    names = ("w1", "b1", "wdw", "bdw", "lng", "lnb", "w2", "b2")
    return pl.pallas_call(
        functools.partial(_conv_kernel, tm),
        grid=(b, t // tm),
        in_specs=[
            pl.BlockSpec((1, CONV_HALO, D_MODEL), lambda i, j: (i, jnp.maximum(j * nh - 1, 0), 0)),
            pl.BlockSpec((1, tm, D_MODEL), lambda i, j: (i, j, 0)),
            pl.BlockSpec((1, CONV_HALO, D_MODEL), lambda i, j: (i, jnp.minimum((j + 1) * nh, last), 0)),
            pl.BlockSpec((1, N_MOD, D_MODEL), lambda i, j: (cond_of_batch(i), 0, 0)),
            _const_spec((1, D_MODEL)),
        ] + [_const_spec(w[n].shape) for n in names],
        out_specs=pl.BlockSpec((1, tm, D_MODEL), lambda i, j: (i, j, 0)),
        out_shape=jax.ShapeDtypeStruct(x.shape, F32),
        scratch_shapes=[
            pltpu.VMEM((tm + 2 * CONV_HALO, D_MODEL), F32),
            pltpu.VMEM((tm, D_MODEL), F32),
        ],
        compiler_params=pltpu.CompilerParams(
            dimension_semantics=("arbitrary", "arbitrary"), vmem_limit_bytes=VMEM_LIMIT),
        name="conv_module",
    )(x, x, x, mod, gmix, *[w[n] for n in names])


def _ffn_kernel(has_o, *refs):
    if has_o:
        x_ref, o_in_ref, wo_ref, mod_ref, gffn_ref, wg_ref, wu_ref, wd_ref, out_ref = refs
    else:
        x_ref, mod_ref, gffn_ref, wg_ref, wu_ref, wd_ref, out_ref = refs
    m = mod_ref[0]
    x = x_ref[...]
    if has_o:
        x = x + m[2:3] * _dot(o_in_ref[...], wo_ref[...])
    h = _mod_norm(x, gffn_ref[...], m[4:5], m[3:4]).astype(BF16)
    acc = jnp.zeros(x.shape, F32)
    for c in range(D_FF // FF_CHUNK):
        cs = slice(c * FF_CHUNK, (c + 1) * FF_CHUNK)
        a = _dot(h, wg_ref[:, cs])
        u = _dot(h, wu_ref[:, cs])
        acc = acc + _dot((_silu(a) * u).astype(BF16), wd_ref[cs, :])
    out_ref[...] = x + m[5:6] * acc


def _ffn(x, o, wo, mod, cond_of_block, gffn, w, tm):
    t = x.shape[0]
    tok_spec = pl.BlockSpec((tm, D_MODEL), lambda i: (i, 0))
    single = pl.Buffered(1)
    in_specs = [tok_spec]
    args = [x]
    if o is not None:
        in_specs += [tok_spec, pl.BlockSpec(wo.shape, lambda i: (0, 0), pipeline_mode=single)]
        args += [o, wo]
    in_specs += [
        pl.BlockSpec((1, N_MOD, D_MODEL), lambda i: (cond_of_block(i), 0, 0)),
        _const_spec((1, D_MODEL)),
        pl.BlockSpec((D_MODEL, D_FF), lambda i: (0, 0), pipeline_mode=single),
        pl.BlockSpec((D_MODEL, D_FF), lambda i: (0, 0), pipeline_mode=single),
        pl.BlockSpec((D_FF, D_MODEL), lambda i: (0, 0), pipeline_mode=single),
    ]
    args += [mod, gffn, w["wg"], w["wu"], w["wd"]]
    return pl.pallas_call(
        functools.partial(_ffn_kernel, o is not None),
        grid=(t // tm,),
        in_specs=in_specs,
        out_specs=tok_spec,
        out_shape=jax.ShapeDtypeStruct(x.shape, F32),
        compiler_params=pltpu.CompilerParams(
            dimension_semantics=("arbitrary",), vmem_limit_bytes=VMEM_LIMIT),
        name="ffn",
    )(*args)


def _rope_tables(t):
    rows = t // GRID_W
    tt = jnp.arange(t)
    row = jnp.repeat(jnp.arange(rows), GRID_W).astype(F32)
    col = (tt % GRID_W).astype(F32)
    ax = QK_ROPE // 2
    inv = 1.0 / (ROPE_THETA ** (jnp.arange(0, ax, 2, dtype=F32) / ax))
    ar = row[:, None] * inv[None, :]
    ac = col[:, None] * inv[None, :]
    cos = jnp.concatenate([jnp.cos(ar), jnp.cos(ar), jnp.cos(ac), jnp.cos(ac)], axis=-1)
    sin = jnp.concatenate([jnp.sin(ar), jnp.sin(ar), jnp.sin(ac), jnp.sin(ac)], axis=-1)
    q = QK_ROPE // 4
    lane = jnp.arange(QK_ROPE)
    first_half = (lane // q) % 2 == 0
    sin_a = jnp.where(first_half, -sin, 0.0)
    sin_b = jnp.where(first_half, 0.0, sin)
    pad = ((0, 0), (0, PE_PAD - QK_ROPE))
    return jnp.pad(cos, pad), jnp.pad(sin_a, pad), jnp.pad(sin_b, pad)


def _row(v):
    return v.reshape(1, -1)


def kernel(x_prompt, x_sample, cache_ckv, cache_kpe, c, c_ctx, w_mod, b_mod, norm_mix, norm_ffn, mla_w_dq, mla_q_norm, mla_w_uq, mla_w_dkv, mla_kv_norm, mla_w_ukv, mla_q_hnorm, mla_k_hnorm, mla_w_o, conv_w_pw1, conv_b_pw1, conv_w_dw, conv_b_dw, conv_ln_g, conv_ln_b, conv_w_pw2, conv_b_pw2, ffn_w_gate, ffn_w_up, ffn_w_down):
    bp, tp, _ = x_prompt.shape
    bs, ts, _ = x_sample.shape
    n_mla = mla_w_dq.shape[0]

    cond = jnp.zeros((COND_ROWS, D_MODEL), F32).at[0].set(c_ctx).at[1:1 + bs].set(c)
    mod_all = _modulation(cond, w_mod, b_mod).reshape(DEPTH, COND_ROWS, N_MOD, D_MODEL)

    rope = _rope_tables(ts)
    cache_kpe_pad = jnp.pad(cache_kpe, ((0, 0), (0, 0), (0, 0), (0, PE_PAD - QK_ROPE)))

    ffn_tm = 512
    blocks_per_sample = ts // ffn_tm
    cond_prompt = lambda i: 0
    cond_sample_blk = lambda i: 1 + i // blocks_per_sample
    cond_sample_batch = lambda i: 1 + i

    xp = x_prompt
    xs = x_sample
    new_ckv, new_kpe = [], []
    for l in range(DEPTH):
        mod = mod_all[l]
        gmix = _row(norm_mix[l])
        gffn = _row(norm_ffn[l])
        fw = {"wg": ffn_w_gate[l].astype(BF16), "wu": ffn_w_up[l].astype(BF16),
              "wd": ffn_w_down[l].astype(BF16)}
        j = l // 2
        if l % 2 == 0:
            wuq = mla_w_uq[j].reshape(Q_LORA, N_HEADS, QK_DIM)
            wuq = jnp.pad(wuq, ((0, 0), (0, 0), (0, HEAD_PAD - QK_DIM)))
            mw = {
                "wdq": mla_w_dq[j].astype(BF16),
                "qn": _row(mla_q_norm[j]),
                "wuq": wuq.reshape(Q_LORA, N_HEADS * HEAD_PAD).astype(BF16),
                "wdkv": jnp.pad(mla_w_dkv[j], ((0, 0), (0, PE_PAD - QK_ROPE))).astype(BF16),
                "kvn": _row(mla_kv_norm[j]),
                "wukv": mla_w_ukv[j].astype(BF16),
                "qhn": _row(jnp.pad(mla_q_hnorm[j], (0, HEAD_PAD - QK_DIM))),
                "khn": _row(jnp.pad(mla_k_hnorm[j], (0, HEAD_PAD - QK_DIM))),
            }
            wo = mla_w_o[j].astype(BF16)
            qp, kp, vp, ckv_p, kpe_p = _mla_proj_prompt(xp, mod, gmix, mw)
            new_ckv.append(ckv_p)
            new_kpe.append(kpe_p)
            op = _attention(qp, kp, vp, tp)
            k_all, v_all = _cache_expand(cache_ckv, cache_kpe_pad, j, mw, ts)
            qs, k_all, v_all = _mla_proj_sample(xs, mod, gmix, mw, rope, k_all, v_all)
            os_ = _attention(qs, k_all, v_all, 256)
            xp2 = _ffn(xp.reshape(bp * tp, D_MODEL), op.reshape(bp * tp, D_MODEL), wo, mod,
                       cond_prompt, gffn, fw, ffn_tm)
            xs2 = _ffn(xs.reshape(bs * ts, D_MODEL), os_.reshape(bs * ts, D_MODEL), wo, mod,
                       cond_sample_blk, gffn, fw, ffn_tm)
        else:
            cw = {
                "w1": conv_w_pw1[j].astype(BF16), "b1": _row(conv_b_pw1[j]),
                "wdw": conv_w_dw[j], "bdw": _row(conv_b_dw[j]),
                "lng": _row(conv_ln_g[j]), "lnb": _row(conv_ln_b[j]),
                "w2": conv_w_pw2[j].astype(BF16), "b2": _row(conv_b_pw2[j]),
            }
            xp1 = _conv_module(xp, mod, cond_prompt, gmix, cw, tp)
            xs1 = _conv_module(xs, mod, cond_sample_batch, gmix, cw, 512)
            xp2 = _ffn(xp1.reshape(bp * tp, D_MODEL), None, None, mod, cond_prompt, gffn, fw, ffn_tm)
            xs2 = _ffn(xs1.reshape(bs * ts, D_MODEL), None, None, mod, cond_sample_blk, gffn, fw,
                       ffn_tm)
        xp = xp2.reshape(bp, tp, D_MODEL)
        xs = xs2.reshape(bs, ts, D_MODEL)

    state_ckv = jnp.stack(new_ckv, axis=1)
    state_kpe = jnp.stack(new_kpe, axis=1)
    return (xp, xs, state_ckv, state_kpe)
```

```python
import functools

import jax
import jax.numpy as jnp
from jax import lax
from jax.experimental import pallas as pl
from jax.experimental.pallas import tpu as pltpu

F32 = jnp.float32
BF16 = jnp.bfloat16

D_MODEL = 1024
DEPTH = 4
N_MOD = 6
GRID_W = 64
N_HEADS = 8
Q_LORA = 512
KV_LORA = 256
QK_NOPE = 128
QK_ROPE = 64
QK_DIM = QK_NOPE + QK_ROPE
V_DIM = 128
HEAD_PAD = 256
PE_PAD = HEAD_PAD - QK_NOPE
ROPE_THETA = 10000.0
CONV_WIDTH = 31
CONV_PAD = (CONV_WIDTH - 1) // 2
CONV_HALO = 16
D_FF = 2816
FF_CHUNK = 1408
EPS = 1e-6
COND_ROWS = 8

VMEM_LIMIT = 56 * 1024 * 1024


def _dot(a, b):
    return jnp.dot(a, b, preferred_element_type=F32)


def _silu(a):
    return a / (1.0 + jnp.exp(-a))


def _rms(x, n):
    return x * lax.rsqrt(jnp.sum(x * x, axis=-1, keepdims=True) * (1.0 / n) + EPS)


def _mod_norm(x, gain, scale, shift):
    return (_rms(x, x.shape[-1]) * gain) * (1.0 + scale) + shift


def _split_bf16(a):
    hi = a.astype(BF16)
    lo = (a - hi.astype(F32)).astype(BF16)
    return hi, lo


def _mod_kernel(c_ref, w_ref, b_ref, o_ref):
    s_hi, s_lo = _split_bf16(_silu(c_ref[...]))
    w_hi, w_lo = _split_bf16(w_ref[0])
    o_ref[0] = _dot(s_hi, w_hi) + _dot(s_lo, w_hi) + _dot(s_hi, w_lo) + b_ref[0]


def _modulation(cond, w_mod, b_mod):
    tn = 1536
    n_out = N_MOD * D_MODEL
    return pl.pallas_call(
        _mod_kernel,
        grid=(DEPTH, n_out // tn),
        in_specs=[
            pl.BlockSpec((COND_ROWS, D_MODEL), lambda l, n: (0, 0)),
            pl.BlockSpec((1, D_MODEL, tn), lambda l, n: (l, 0, n)),
            pl.BlockSpec((1, 1, tn), lambda l, n: (l, 0, n)),
        ],
        out_specs=pl.BlockSpec((1, COND_ROWS, tn), lambda l, n: (l, 0, n)),
        out_shape=jax.ShapeDtypeStruct((DEPTH, COND_ROWS, n_out), F32),
        compiler_params=pltpu.CompilerParams(
            dimension_semantics=("arbitrary", "arbitrary"), vmem_limit_bytes=VMEM_LIMIT),
        name="modulation",
    )(cond, w_mod, b_mod.reshape(DEPTH, 1, n_out))


def _expand_kv(ckv, kpe_rot, kpe_ss, wukv_ref, khn_ref, k_ref, v_ref):
    kvu = _dot(ckv.astype(BF16), wukv_ref[...])
    gain_nope = khn_ref[:, :QK_NOPE]
    for h in range(N_HEADS):
        kn = kvu[:, h * 256:h * 256 + QK_NOPE]
        ss = jnp.sum(kn * kn, axis=-1, keepdims=True) + kpe_ss
        inv = lax.rsqrt(ss * (1.0 / QK_DIM) + EPS)
        k_ref[0, h, :, :QK_NOPE] = (kn * inv * gain_nope).astype(BF16)
        k_ref[0, h, :, QK_NOPE:] = (kpe_rot * inv).astype(BF16)
        v_ref[0, h] = kvu[:, h * 256 + QK_NOPE:(h + 1) * 256].astype(BF16)


def _rope(pe, cos, sin_a, sin_b):
    return (pe * cos + pltpu.roll(pe, PE_PAD - QK_ROPE // 4, axis=1) * sin_a
            + pltpu.roll(pe, QK_ROPE // 4, axis=1) * sin_b)


def _mla_proj_kernel(use_rope, emit_state, *refs):
    refs = list(refs)
    x_ref, mod_ref, gmix_ref, wdq_ref, qn_ref, wuq_ref, wdkv_ref, kvn_ref, wukv_ref, qhn_ref, khn_ref = refs[:11]
    refs = refs[11:]
    if use_rope:
        cos_ref, sina_ref, sinb_ref = refs[:3]
        refs = refs[5:]
    q_ref, k_ref, v_ref = refs[:3]
    refs = refs[3:]

    m = mod_ref[0]
    h = _mod_norm(x_ref[0], gmix_ref[...], m[1:2], m[0:1]).astype(BF16)

    kv = _dot(h, wdkv_ref[...])
    ckv = _rms(kv[:, :KV_LORA], KV_LORA) * kvn_ref[...]
    kpe = kv[:, KV_LORA:]
    if emit_state:
        ckv_ref, kpe_ref = refs
        ckv_ref[0] = ckv
        kpe_ref[0] = kpe[:, :QK_ROPE]
    kpe_ss = jnp.sum(kpe * kpe, axis=-1, keepdims=True)
    kpe_rot = kpe * khn_ref[:, QK_NOPE:]
    if use_rope:
        cos, sin_a, sin_b = cos_ref[...], sina_ref[...], sinb_ref[...]
        kpe_rot = _rope(kpe_rot, cos, sin_a, sin_b)
    _expand_kv(ckv, kpe_rot, kpe_ss, wukv_ref, khn_ref, k_ref, v_ref)

    cq = (_rms(_dot(h, wdq_ref[...]), Q_LORA) * qn_ref[...]).astype(BF16)
    qf = _dot(cq, wuq_ref[...])
    scale = QK_DIM ** -0.5
    for hd in range(N_HEADS):
        qh = qf[:, hd * HEAD_PAD:(hd + 1) * HEAD_PAD]
        inv = lax.rsqrt(jnp.sum(qh * qh, axis=-1, keepdims=True) * (1.0 / QK_DIM) + EPS) * scale
        qn = qh * inv * qhn_ref[...]
        q_ref[0, hd, :, :QK_NOPE] = qn[:, :QK_NOPE].astype(BF16)
        qpe = qn[:, QK_NOPE:]
        if use_rope:
            qpe = _rope(qpe, cos, sin_a, sin_b)
        q_ref[0, hd, :, QK_NOPE:] = qpe.astype(BF16)


def _cache_expand_kernel(ckv_ref, kpe_ref, wukv_ref, khn_ref, k_ref, v_ref):
    kpe = kpe_ref[0]
    kpe_ss = jnp.sum(kpe * kpe, axis=-1, keepdims=True)
    _expand_kv(ckv_ref[0], kpe * khn_ref[:, QK_NOPE:], kpe_ss, wukv_ref, khn_ref, k_ref, v_ref)


def _const_spec(shape):
    nd = len(shape)
    return pl.BlockSpec(shape, lambda *_: (0,) * nd)


def _mla_weight_specs(w):
    return [_const_spec(w[n].shape) for n in
            ("wdq", "qn", "wuq", "wdkv", "kvn", "wukv", "qhn", "khn")]


def _mla_weight_args(w):
    return [w[n] for n in ("wdq", "qn", "wuq", "wdkv", "kvn", "wukv", "qhn", "khn")]


def _mla_proj_prompt(x, mod, gmix, w):
    b, t, _ = x.shape
    tm = t
    kern = functools.partial(_mla_proj_kernel, False, True)
    return pl.pallas_call(
        kern,
        grid=(b, t // tm),
        in_specs=[
            pl.BlockSpec((1, tm, D_MODEL), lambda i, j: (i, j, 0)),
            pl.BlockSpec((1, N_MOD, D_MODEL), lambda i, j: (0, 0, 0)),
            _const_spec((1, D_MODEL)),
        ] + _mla_weight_specs(w),
        out_specs=[
            pl.BlockSpec((1, N_HEADS, tm, HEAD_PAD), lambda i, j: (i, 0, j, 0)),
            pl.BlockSpec((1, N_HEADS, tm, HEAD_PAD), lambda i, j: (i, 0, j, 0)),
            pl.BlockSpec((1, N_HEADS, tm, V_DIM), lambda i, j: (i, 0, j, 0)),
            pl.BlockSpec((1, tm, KV_LORA), lambda i, j: (i, j, 0)),
            pl.BlockSpec((1, tm, QK_ROPE), lambda i, j: (i, j, 0)),
        ],
        out_shape=[
            jax.ShapeDtypeStruct((b, N_HEADS, t, HEAD_PAD), BF16),
            jax.ShapeDtypeStruct((b, N_HEADS, t, HEAD_PAD), BF16),
            jax.ShapeDtypeStruct((b, N_HEADS, t, V_DIM), BF16),
            jax.ShapeDtypeStruct((b, t, KV_LORA), F32),
            jax.ShapeDtypeStruct((b, t, QK_ROPE), F32),
        ],
        compiler_params=pltpu.CompilerParams(
            dimension_semantics=("arbitrary", "arbitrary"), vmem_limit_bytes=VMEM_LIMIT),
        name="mla_proj_prompt",
    )(x, mod, gmix, *_mla_weight_args(w))


def _mla_proj_sample(x, mod, gmix, w, rope, k_all, v_all):
    b, t, _ = x.shape
    tm = 512
    kern = functools.partial(_mla_proj_kernel, True, False)
    rope_spec = pl.BlockSpec((tm, PE_PAD), lambda i, j: (j, 0))
    return pl.pallas_call(
        kern,
        grid=(b, t // tm),
        in_specs=[
            pl.BlockSpec((1, tm, D_MODEL), lambda i, j: (i, j, 0)),
            pl.BlockSpec((1, N_MOD, D_MODEL), lambda i, j: (i + 1, 0, 0)),
            _const_spec((1, D_MODEL)),
        ] + _mla_weight_specs(w) + [
            rope_spec, rope_spec, rope_spec,
            pl.BlockSpec(memory_space=pl.ANY),
            pl.BlockSpec(memory_space=pl.ANY),
        ],
        out_specs=[
            pl.BlockSpec((1, N_HEADS, tm, HEAD_PAD), lambda i, j: (i, 0, j, 0)),
            pl.BlockSpec((1, N_HEADS, tm, HEAD_PAD), lambda i, j: (i, 0, j, 0)),
            pl.BlockSpec((1, N_HEADS, tm, V_DIM), lambda i, j: (i, 0, j, 0)),
        ],
        out_shape=[
            jax.ShapeDtypeStruct((b, N_HEADS, t, HEAD_PAD), BF16),
            jax.ShapeDtypeStruct(k_all.shape, BF16),
            jax.ShapeDtypeStruct(v_all.shape, BF16),
        ],
        input_output_aliases={14: 1, 15: 2},
        compiler_params=pltpu.CompilerParams(
            dimension_semantics=("arbitrary", "arbitrary"), vmem_limit_bytes=VMEM_LIMIT),
        name="mla_proj_sample",
    )(x, mod, gmix, *_mla_weight_args(w), *rope, k_all, v_all)


def _cache_expand(cache_ckv, cache_kpe_pad, j, w, t_latent):
    b, _, past, _ = cache_ckv.shape
    blk = t_latent // past
    t_all = t_latent + past
    return pl.pallas_call(
        _cache_expand_kernel,
        grid=(b,),
        in_specs=[
            pl.BlockSpec((1, None, past, KV_LORA), lambda i: (i, j, 0, 0)),
            pl.BlockSpec((1, None, past, PE_PAD), lambda i: (i, j, 0, 0)),
            _const_spec(w["wukv"].shape),
            _const_spec(w["khn"].shape),
        ],
        out_specs=[
            pl.BlockSpec((1, N_HEADS, past, HEAD_PAD), lambda i: (i, 0, blk, 0)),
            pl.BlockSpec((1, N_HEADS, past, V_DIM), lambda i: (i, 0, blk, 0)),
        ],
        out_shape=[
            jax.ShapeDtypeStruct((b, N_HEADS, t_all, HEAD_PAD), BF16),
            jax.ShapeDtypeStruct((b, N_HEADS, t_all, V_DIM), BF16),
        ],
        compiler_params=pltpu.CompilerParams(
            dimension_semantics=("arbitrary",), vmem_limit_bytes=VMEM_LIMIT),
        name="cache_expand",
    )(cache_ckv, cache_kpe_pad, w["wukv"], w["khn"])


def _key_chunks(tk, chunk):
    return [(s0, min(chunk, tk - s0)) for s0 in range(0, tk, chunk)]


def _attn_kernel(heads, chunks, q_ref, k_ref, v_ref, o_ref):
    for h in range(heads):
        q = q_ref[0, h]
        m = l = acc = None
        for s0, size in chunks:
            s = lax.dot_general(q, k_ref[0, h, s0:s0 + size, :], (((1,), (1,)), ((), ())),
                                preferred_element_type=F32)
            m_c = jnp.max(s, axis=-1, keepdims=True)
            if m is None:
                m = m_c
                p = jnp.exp(s - m)
                l = jnp.sum(p, axis=-1, keepdims=True)
                acc = _dot(p.astype(BF16), v_ref[0, h, s0:s0 + size, :])
            else:
                m_new = jnp.maximum(m, m_c)
                alpha = jnp.exp(m - m_new)
                p = jnp.exp(s - m_new)
                l = alpha * l + jnp.sum(p, axis=-1, keepdims=True)
                acc = alpha * acc + _dot(p.astype(BF16), v_ref[0, h, s0:s0 + size, :])
                m = m_new
        o_ref[0, :, h * V_DIM:(h + 1) * V_DIM] = (acc / l).astype(BF16)


def _attention(q, k, v, tq, heads, chunk):
    b, _, t, _ = q.shape
    tk = k.shape[2]
    kern = functools.partial(_attn_kernel, heads, _key_chunks(tk, chunk))
    return pl.pallas_call(
        kern,
        grid=(b, N_HEADS // heads, t // tq),
        in_specs=[
            pl.BlockSpec((1, heads, tq, HEAD_PAD), lambda i, h, j: (i, h, j, 0)),
            pl.BlockSpec((1, heads, tk, HEAD_PAD), lambda i, h, j: (i, h, 0, 0)),
            pl.BlockSpec((1, heads, tk, V_DIM), lambda i, h, j: (i, h, 0, 0)),
        ],
        out_specs=pl.BlockSpec((1, tq, heads * V_DIM), lambda i, h, j: (i, j, h)),
        out_shape=jax.ShapeDtypeStruct((b, t, N_HEADS * V_DIM), BF16),
        compiler_params=pltpu.CompilerParams(
            dimension_semantics=("arbitrary", "arbitrary", "arbitrary"),
            vmem_limit_bytes=VMEM_LIMIT),
        name="attention",
    )(q, k, v)


def _conv_kernel(tm, xp_ref, x_ref, xn_ref, mod_ref, gmix_ref, w1_ref, b1_ref, wdw_ref, bdw_ref,
                 lng_ref, lnb_ref, w2_ref, b2_ref, o_ref, gflat_ref, gt_ref, yflat_ref):
    j = pl.program_id(1)
    rows = tm + 2 * CONV_HALO
    m = mod_ref[0]
    x = x_ref[0]
    x_ext = jnp.concatenate([xp_ref[0], x, xn_ref[0]], axis=0)
    h = _mod_norm(x_ext, gmix_ref[...], m[1:2], m[0:1]).astype(BF16)
    a = _dot(h, w1_ref[...]) + b1_ref[...]
    g = a[:, :D_MODEL] / (1.0 + jnp.exp(-a[:, D_MODEL:]))
    r = lax.broadcasted_iota(jnp.int32, (rows, 1), 0)
    valid = jnp.logical_and(jnp.logical_or(r >= CONV_HALO, j > 0),
                            jnp.logical_or(r < tm + CONV_HALO, j < pl.num_programs(1) - 1))
    g = jnp.where(valid, g, 0.0)

    n_tiles = D_MODEL // 128
    gs = rows + 8
    ys = tm + 8
    for c in range(n_tiles):
        gflat_ref[c * gs:c * gs + rows, :] = g[:, c * 128:(c + 1) * 128]

    def to_time_major(r, carry):
        gt_ref[r] = gflat_ref[pl.ds(r, n_tiles, stride=gs), :]
        return carry
    lax.fori_loop(0, rows, to_time_major, 0, unroll=8)

    first = CONV_HALO - CONV_PAD
    tb = 16
    bias = bdw_ref[...]

    def conv_block(i, carry):
        t0 = pl.multiple_of(i * tb, tb)
        accs = [bias] * tb
        for k in range(CONV_WIDTH):
            wk = wdw_ref[k]
            for u in range(tb):
                accs[u] = accs[u] + gt_ref[t0 + (u + first + k)] * wk
        for u in range(tb):
            yflat_ref[pl.ds(t0 + u, n_tiles, stride=ys), :] = accs[u]
        return carry
    lax.fori_loop(0, tm // tb, conv_block, 0)

    y = jnp.concatenate([yflat_ref[c * ys:c * ys + tm, :] for c in range(n_tiles)], axis=-1)
    mu = jnp.mean(y, axis=-1, keepdims=True)
    yc = y - mu
    var = jnp.mean(yc * yc, axis=-1, keepdims=True)
    z = _silu(yc * lax.rsqrt(var + EPS) * lng_ref[...] + lnb_ref[...]).astype(BF16)
    o_ref[0] = x + m[2:3] * (_dot(z, w2_ref[...]) + b2_ref[...])


def _conv_module(x, mod, cond_of_batch, gmix, w, tm):
    b, t, _ = x.shape
    nh = tm // CONV_HALO
    last = t // CONV_HALO - 1
    names = ("w1", "b1", "wdw", "bdw", "lng", "lnb", "w2", "b2")
    return pl.pallas_call(
        functools.partial(_conv_kernel, tm),
        grid=(b, t // tm),
        in_specs=[
            pl.BlockSpec((1, CONV_HALO, D_MODEL), lambda i, j: (i, jnp.maximum(j * nh - 1, 0), 0)),
            pl.BlockSpec((1, tm, D_MODEL), lambda i, j: (i, j, 0)),
            pl.BlockSpec((1, CONV_HALO, D_MODEL), lambda i, j: (i, jnp.minimum((j + 1) * nh, last), 0)),
            pl.BlockSpec((1, N_MOD, D_MODEL), lambda i, j: (cond_of_batch(i), 0, 0)),
            _const_spec((1, D_MODEL)),
        ] + [_const_spec(w[n].shape) for n in names],
        out_specs=pl.BlockSpec((1, tm, D_MODEL), lambda i, j: (i, j, 0)),
        out_shape=jax.ShapeDtypeStruct(x.shape, F32),
        scratch_shapes=[
            pltpu.VMEM((D_MODEL // 128 * (tm + 2 * CONV_HALO + 8), 128), F32),
            pltpu.VMEM((tm + 2 * CONV_HALO, D_MODEL // 128, 128), F32),
            pltpu.VMEM((D_MODEL // 128 * (tm + 8), 128), F32),
        ],
        compiler_params=pltpu.CompilerParams(
            dimension_semantics=("arbitrary", "arbitrary"), vmem_limit_bytes=VMEM_LIMIT),
        name="conv_module",
    )(x, x, x, mod, gmix, *[w[n] for n in names])


def _ffn_kernel(has_o, *refs):
    if has_o:
        x_ref, o_in_ref, wo_ref, mod_ref, gffn_ref, wg_ref, wu_ref, wd_ref, out_ref = refs
    else:
        x_ref, mod_ref, gffn_ref, wg_ref, wu_ref, wd_ref, out_ref = refs
    m = mod_ref[0]
    x = x_ref[...]
    if has_o:
        x = x + m[2:3] * _dot(o_in_ref[...], wo_ref[...])
    h = _mod_norm(x, gffn_ref[...], m[4:5], m[3:4]).astype(BF16)
    acc = jnp.zeros(x.shape, F32)
    for c in range(D_FF // FF_CHUNK):
        cs = slice(c * FF_CHUNK, (c + 1) * FF_CHUNK)
        a = _dot(h, wg_ref[:, cs])
        u = _dot(h, wu_ref[:, cs])
        acc = acc + _dot((_silu(a) * u).astype(BF16), wd_ref[cs, :])
    out_ref[...] = x + m[5:6] * acc


def _ffn(x, o, wo, mod, cond_of_block, gffn, w, tm):
    t = x.shape[0]
    tok_spec = pl.BlockSpec((tm, D_MODEL), lambda i: (i, 0))
    single = pl.Buffered(1)
    in_specs = [tok_spec]
    args = [x]
    if o is not None:
        in_specs += [tok_spec, pl.BlockSpec(wo.shape, lambda i: (0, 0), pipeline_mode=single)]
        args += [o, wo]
    in_specs += [
        pl.BlockSpec((1, N_MOD, D_MODEL), lambda i: (cond_of_block(i), 0, 0)),
        _const_spec((1, D_MODEL)),
        pl.BlockSpec((D_MODEL, D_FF), lambda i: (0, 0), pipeline_mode=single),
        pl.BlockSpec((D_MODEL, D_FF), lambda i: (0, 0), pipeline_mode=single),
        pl.BlockSpec((D_FF, D_MODEL), lambda i: (0, 0), pipeline_mode=single),
    ]
    args += [mod, gffn, w["wg"], w["wu"], w["wd"]]
    return pl.pallas_call(
        functools.partial(_ffn_kernel, o is not None),
        grid=(t // tm,),
        in_specs=in_specs,
        out_specs=tok_spec,
        out_shape=jax.ShapeDtypeStruct(x.shape, F32),
        compiler_params=pltpu.CompilerParams(
            dimension_semantics=("arbitrary",), vmem_limit_bytes=VMEM_LIMIT),
        name="ffn",
    )(*args)


def _rope_tables(t):
    rows = t // GRID_W
    tt = jnp.arange(t)
    row = jnp.repeat(jnp.arange(rows), GRID_W).astype(F32)
    col = (tt % GRID_W).astype(F32)
    ax = QK_ROPE // 2
    inv = 1.0 / (ROPE_THETA ** (jnp.arange(0, ax, 2, dtype=F32) / ax))
    ar = row[:, None] * inv[None, :]
    ac = col[:, None] * inv[None, :]
    cos = jnp.concatenate([jnp.cos(ar), jnp.cos(ar), jnp.cos(ac), jnp.cos(ac)], axis=-1)
    sin = jnp.concatenate([jnp.sin(ar), jnp.sin(ar), jnp.sin(ac), jnp.sin(ac)], axis=-1)
    q = QK_ROPE // 4
    lane = jnp.arange(QK_ROPE)
    first_half = (lane // q) % 2 == 0
    sin_a = jnp.where(first_half, -sin, 0.0)
    sin_b = jnp.where(first_half, 0.0, sin)
    pad = ((0, 0), (0, PE_PAD - QK_ROPE))
    return jnp.pad(cos, pad), jnp.pad(sin_a, pad), jnp.pad(sin_b, pad)


def _row(v):
    return v.reshape(1, -1)


def kernel(x_prompt, x_sample, cache_ckv, cache_kpe, c, c_ctx, w_mod, b_mod, norm_mix, norm_ffn, mla_w_dq, mla_q_norm, mla_w_uq, mla_w_dkv, mla_kv_norm, mla_w_ukv, mla_q_hnorm, mla_k_hnorm, mla_w_o, conv_w_pw1, conv_b_pw1, conv_w_dw, conv_b_dw, conv_ln_g, conv_ln_b, conv_w_pw2, conv_b_pw2, ffn_w_gate, ffn_w_up, ffn_w_down):
    bp, tp, _ = x_prompt.shape
    bs, ts, _ = x_sample.shape
    n_mla = mla_w_dq.shape[0]

    cond = jnp.zeros((COND_ROWS, D_MODEL), F32).at[0].set(c_ctx).at[1:1 + bs].set(c)
    mod_all = _modulation(cond, w_mod, b_mod).reshape(DEPTH, COND_ROWS, N_MOD, D_MODEL)

    rope = _rope_tables(ts)
    cache_kpe_pad = jnp.pad(cache_kpe, ((0, 0), (0, 0), (0, 0), (0, PE_PAD - QK_ROPE)))

    ffn_tm = 512
    blocks_per_sample = ts // ffn_tm
    cond_prompt = lambda i: 0
    cond_sample_blk = lambda i: 1 + i // blocks_per_sample
    cond_sample_batch = lambda i: 1 + i

    xp = x_prompt
    xs = x_sample
    new_ckv, new_kpe = [], []
    for l in range(DEPTH):
        mod = mod_all[l]
        gmix = _row(norm_mix[l])
        gffn = _row(norm_ffn[l])
        fw = {"wg": ffn_w_gate[l].astype(BF16), "wu": ffn_w_up[l].astype(BF16),
              "wd": ffn_w_down[l].astype(BF16)}
        j = l // 2
        if l % 2 == 0:
            wuq = mla_w_uq[j].reshape(Q_LORA, N_HEADS, QK_DIM)
            wuq = jnp.pad(wuq, ((0, 0), (0, 0), (0, HEAD_PAD - QK_DIM)))
            mw = {
                "wdq": mla_w_dq[j].astype(BF16),
                "qn": _row(mla_q_norm[j]),
                "wuq": wuq.reshape(Q_LORA, N_HEADS * HEAD_PAD).astype(BF16),
                "wdkv": jnp.pad(mla_w_dkv[j], ((0, 0), (0, PE_PAD - QK_ROPE))).astype(BF16),
                "kvn": _row(mla_kv_norm[j]),
                "wukv": mla_w_ukv[j].astype(BF16),
                "qhn": _row(jnp.pad(mla_q_hnorm[j], (0, HEAD_PAD - QK_DIM))),
                "khn": _row(jnp.pad(mla_k_hnorm[j], (0, HEAD_PAD - QK_DIM))),
            }
            wo = mla_w_o[j].astype(BF16)
            qp, kp, vp, ckv_p, kpe_p = _mla_proj_prompt(xp, mod, gmix, mw)
            new_ckv.append(ckv_p)
            new_kpe.append(kpe_p)
            op = _attention(qp, kp, vp, tp, N_HEADS, tp)
            k_all, v_all = _cache_expand(cache_ckv, cache_kpe_pad, j, mw, ts)
            qs, k_all, v_all = _mla_proj_sample(xs, mod, gmix, mw, rope, k_all, v_all)
            os_ = _attention(qs, k_all, v_all, 512, 2, 1024)
            xp2 = _ffn(xp.reshape(bp * tp, D_MODEL), op.reshape(bp * tp, D_MODEL), wo, mod,
                       cond_prompt, gffn, fw, ffn_tm)
            xs2 = _ffn(xs.reshape(bs * ts, D_MODEL), os_.reshape(bs * ts, D_MODEL), wo, mod,
                       cond_sample_blk, gffn, fw, ffn_tm)
        else:
            cw = {
                "w1": conv_w_pw1[j].astype(BF16), "b1": _row(conv_b_pw1[j]),
                "wdw": conv_w_dw[j].reshape(CONV_WIDTH, D_MODEL // 128, 128),
                "bdw": conv_b_dw[j].reshape(D_MODEL // 128, 128),
                "lng": _row(conv_ln_g[j]), "lnb": _row(conv_ln_b[j]),
                "w2": conv_w_pw2[j].astype(BF16), "b2": _row(conv_b_pw2[j]),
            }
            xp1 = _conv_module(xp, mod, cond_prompt, gmix, cw, tp)
            xs1 = _conv_module(xs, mod, cond_sample_batch, gmix, cw, 512)
            xp2 = _ffn(xp1.reshape(bp * tp, D_MODEL), None, None, mod, cond_prompt, gffn, fw, ffn_tm)
            xs2 = _ffn(xs1.reshape(bs * ts, D_MODEL), None, None, mod, cond_sample_blk, gffn, fw,
                       ffn_tm)
        xp = xp2.reshape(bp, tp, D_MODEL)
        xs = xs2.reshape(bs, ts, D_MODEL)

    state_ckv = jnp.stack(new_ckv, axis=1)
    state_kpe = jnp.stack(new_kpe, axis=1)
    return (xp, xs, state_ckv, state_kpe)
```

```python
import functools

import jax
import jax.numpy as jnp
from jax import lax
from jax.experimental import pallas as pl
from jax.experimental.pallas import tpu as pltpu

F32 = jnp.float32
BF16 = jnp.bfloat16

D_MODEL = 1024
DEPTH = 4
N_MOD = 6
GRID_W = 64
N_HEADS = 8
Q_LORA = 512
KV_LORA = 256
QK_NOPE = 128
QK_ROPE = 64
QK_DIM = QK_NOPE + QK_ROPE
V_DIM = 128
HEAD_PAD = 256
PE_PAD = HEAD_PAD - QK_NOPE
ROPE_THETA = 10000.0
CONV_WIDTH = 31
CONV_PAD = (CONV_WIDTH - 1) // 2
CONV_HALO = 16
D_FF = 2816
FF_CHUNK = 1408
EPS = 1e-6
LOG2_E = 1.4426950408889634
COND_ROWS = 8

VMEM_LIMIT = 56 * 1024 * 1024


def _dot(a, b):
    return jnp.dot(a, b, preferred_element_type=F32)


def _silu(a):
    return a / (1.0 + jnp.exp(-a))


def _rms(x, n):
    return x * lax.rsqrt(jnp.sum(x * x, axis=-1, keepdims=True) * (1.0 / n) + EPS)


def _mod_norm(x, gain, scale, shift):
    return _rms(x, x.shape[-1]) * (gain * (1.0 + scale)) + shift


def _split_bf16(a):
    hi = a.astype(BF16)
    lo = (a - hi.astype(F32)).astype(BF16)
    return hi, lo


def _mod_kernel(c_ref, w_ref, b_ref, o_ref):
    s_hi, s_lo = _split_bf16(_silu(c_ref[...]))
    w_hi, w_lo = _split_bf16(w_ref[0])
    o_ref[0] = _dot(s_hi, w_hi) + _dot(s_lo, w_hi) + _dot(s_hi, w_lo) + b_ref[0]


def _modulation(cond, w_mod, b_mod):
    tn = 1536
    n_out = N_MOD * D_MODEL
    return pl.pallas_call(
        _mod_kernel,
        grid=(DEPTH, n_out // tn),
        in_specs=[
            pl.BlockSpec((COND_ROWS, D_MODEL), lambda l, n: (0, 0)),
            pl.BlockSpec((1, D_MODEL, tn), lambda l, n: (l, 0, n)),
            pl.BlockSpec((1, 1, tn), lambda l, n: (l, 0, n)),
        ],
        out_specs=pl.BlockSpec((1, COND_ROWS, tn), lambda l, n: (l, 0, n)),
        out_shape=jax.ShapeDtypeStruct((DEPTH, COND_ROWS, n_out), F32),
        compiler_params=pltpu.CompilerParams(
            dimension_semantics=("arbitrary", "arbitrary"), vmem_limit_bytes=VMEM_LIMIT),
        name="modulation",
    )(cond, w_mod, b_mod.reshape(DEPTH, 1, n_out))


def _expand_kv(ckv, kpe_rot, kpe_ss, wk_ref, wvt_ref, khn_ref, k_ref, vt_ref):
    ckv_b = ckv.astype(BF16)
    kn_all = _dot(ckv_b, wk_ref[...])
    vt_all = lax.dot_general(wvt_ref[...], ckv_b, (((1,), (1,)), ((), ())),
                             preferred_element_type=F32)
    gain_nope = khn_ref[:, :QK_NOPE]
    for h in range(N_HEADS):
        kn = kn_all[:, h * QK_NOPE:(h + 1) * QK_NOPE]
        ss = jnp.sum(kn * kn, axis=-1, keepdims=True) + kpe_ss
        inv = lax.rsqrt(ss * (1.0 / QK_DIM) + EPS)
        k_ref[0, h, :, :QK_NOPE] = (kn * inv * gain_nope).astype(BF16)
        k_ref[0, h, :, QK_NOPE:] = (kpe_rot * inv).astype(BF16)
        vt_ref[0, h] = vt_all[h * V_DIM:(h + 1) * V_DIM, :].astype(BF16)


def _rope(pe, cos, sin_a, sin_b):
    return (pe * cos + pltpu.roll(pe, PE_PAD - QK_ROPE // 4, axis=1) * sin_a
            + pltpu.roll(pe, QK_ROPE // 4, axis=1) * sin_b)


_MLA_WEIGHTS = ("wdq", "qn", "wuq", "wdkv", "kvn", "wk", "wvt", "qhn", "khn")


def _mla_proj_kernel(use_rope, emit_state, *refs):
    refs = list(refs)
    n_in = 3 + len(_MLA_WEIGHTS)
    (x_ref, mod_ref, gmix_ref, wdq_ref, qn_ref, wuq_ref, wdkv_ref, kvn_ref, wk_ref, wvt_ref,
     qhn_ref, khn_ref) = refs[:n_in]
    refs = refs[n_in:]
    if use_rope:
        cos_ref, sina_ref, sinb_ref = refs[:3]
        refs = refs[3:]
    q_ref, k_ref, vt_ref = refs[:3]
    refs = refs[3:]

    m = mod_ref[0]
    h = _mod_norm(x_ref[0], gmix_ref[...], m[1:2], m[0:1]).astype(BF16)

    kv = _dot(h, wdkv_ref[...])
    ckv = _rms(kv[:, :KV_LORA], KV_LORA) * kvn_ref[...]
    kpe = kv[:, KV_LORA:]
    if emit_state:
        ckv_ref, kpe_ref = refs
        ckv_ref[0] = ckv
        kpe_ref[0] = kpe[:, :QK_ROPE]
    kpe_ss = jnp.sum(kpe * kpe, axis=-1, keepdims=True)
    kpe_rot = kpe * khn_ref[:, QK_NOPE:]
    if use_rope:
        cos, sin_a, sin_b = cos_ref[...], sina_ref[...], sinb_ref[...]
        kpe_rot = _rope(kpe_rot, cos, sin_a, sin_b)
    _expand_kv(ckv, kpe_rot, kpe_ss, wk_ref, wvt_ref, khn_ref, k_ref, vt_ref)

    cq = (_rms(_dot(h, wdq_ref[...]), Q_LORA) * qn_ref[...]).astype(BF16)
    qf = _dot(cq, wuq_ref[...])
    q_gain = qhn_ref[...] * (QK_DIM ** -0.5 * LOG2_E)
    for hd in range(N_HEADS):
        qh = qf[:, hd * HEAD_PAD:(hd + 1) * HEAD_PAD]
        inv = lax.rsqrt(jnp.sum(qh * qh, axis=-1, keepdims=True) * (1.0 / QK_DIM) + EPS)
        qn = qh * inv * q_gain
        q_ref[0, hd, :, :QK_NOPE] = qn[:, :QK_NOPE].astype(BF16)
        qpe = qn[:, QK_NOPE:]
        if use_rope:
            qpe = _rope(qpe, cos, sin_a, sin_b)
        q_ref[0, hd, :, QK_NOPE:] = qpe.astype(BF16)


def _cache_expand_kernel(ckv_ref, kpe_ref, wk_ref, wvt_ref, khn_ref, k_ref, vt_ref):
    kpe = kpe_ref[0]
    kpe_ss = jnp.sum(kpe * kpe, axis=-1, keepdims=True)
    _expand_kv(ckv_ref[0], kpe * khn_ref[:, QK_NOPE:], kpe_ss, wk_ref, wvt_ref, khn_ref,
               k_ref, vt_ref)


def _const_spec(shape):
    nd = len(shape)
    return pl.BlockSpec(shape, lambda *_: (0,) * nd)


def _mla_proj(x, mod, cond_of_batch, gmix, w, tm, rope):
    b, t, _ = x.shape
    use_rope = rope is not None
    kern = functools.partial(_mla_proj_kernel, use_rope, not use_rope)
    in_specs = [
        pl.BlockSpec((1, tm, D_MODEL), lambda i, j: (i, j, 0)),
        pl.BlockSpec((1, N_MOD, D_MODEL), lambda i, j: (cond_of_batch(i), 0, 0)),
        _const_spec((1, D_MODEL)),
    ] + [_const_spec(w[n].shape) for n in _MLA_WEIGHTS]
    args = [x, mod, gmix] + [w[n] for n in _MLA_WEIGHTS]
    out_specs = [
        pl.BlockSpec((1, N_HEADS, tm, HEAD_PAD), lambda i, j: (i, 0, j, 0)),
        pl.BlockSpec((1, N_HEADS, tm, HEAD_PAD), lambda i, j: (i, 0, j, 0)),
        pl.BlockSpec((1, N_HEADS, V_DIM, tm), lambda i, j: (i, 0, 0, j)),
    ]
    out_shape = [
        jax.ShapeDtypeStruct((b, N_HEADS, t, HEAD_PAD), BF16),
        jax.ShapeDtypeStruct((b, N_HEADS, t, HEAD_PAD), BF16),
        jax.ShapeDtypeStruct((b, N_HEADS, V_DIM, t), BF16),
    ]
    if use_rope:
        in_specs += [pl.BlockSpec((tm, PE_PAD), lambda i, j: (j, 0))] * 3
        args += list(rope)
    else:
        out_specs += [pl.BlockSpec((1, tm, KV_LORA), lambda i, j: (i, j, 0)),
                      pl.BlockSpec((1, tm, QK_ROPE), lambda i, j: (i, j, 0))]
        out_shape += [jax.ShapeDtypeStruct((b, t, KV_LORA), F32),
                      jax.ShapeDtypeStruct((b, t, QK_ROPE), F32)]
    return pl.pallas_call(
        kern,
        grid=(b, t // tm),
        in_specs=in_specs,
        out_specs=out_specs,
        out_shape=out_shape,
        compiler_params=pltpu.CompilerParams(
            dimension_semantics=("arbitrary", "arbitrary"), vmem_limit_bytes=VMEM_LIMIT),
        name="mla_proj",
    )(*args)


def _cache_expand(cache_ckv, cache_kpe_pad, j, w):
    b, _, past, _ = cache_ckv.shape
    names = ("wk", "wvt", "khn")
    return pl.pallas_call(
        _cache_expand_kernel,
        grid=(b,),
        in_specs=[
            pl.BlockSpec((1, None, past, KV_LORA), lambda i: (i, j, 0, 0)),
            pl.BlockSpec((1, None, past, PE_PAD), lambda i: (i, j, 0, 0)),
        ] + [_const_spec(w[n].shape) for n in names],
        out_specs=[
            pl.BlockSpec((1, N_HEADS, past, HEAD_PAD), lambda i: (i, 0, 0, 0)),
            pl.BlockSpec((1, N_HEADS, V_DIM, past), lambda i: (i, 0, 0, 0)),
        ],
        out_shape=[
            jax.ShapeDtypeStruct((b, N_HEADS, past, HEAD_PAD), BF16),
            jax.ShapeDtypeStruct((b, N_HEADS, V_DIM, past), BF16),
        ],
        compiler_params=pltpu.CompilerParams(
            dimension_semantics=("arbitrary",), vmem_limit_bytes=VMEM_LIMIT),
        name="cache_expand",
    )(cache_ckv, cache_kpe_pad, *[w[n] for n in names])


def _attn_kernel(heads, chunk, has_cache, *refs):
    if has_cache:
        q_ref, k_ref, vt_ref, kc_ref, vtc_ref, o_ref, s_scr, p_scr = refs
    else:
        q_ref, k_ref, vt_ref, o_ref, s_scr, p_scr = refs
    tk = k_ref.shape[2]
    pieces = [[(k_ref, vt_ref, s0, min(chunk, tk - s0))] for s0 in range(0, tk, chunk)]
    if has_cache:
        pieces[-1].append((kc_ref, vtc_ref, 0, kc_ref.shape[2]))
    steps = [(h, piece) for h in range(heads) for piece in pieces]

    n_slots = s_scr.shape[0]

    def piece_rows(n):
        return sum(size for _, _, _, size in steps[n][1])

    def scores(n):
        h, piece = steps[n]
        row = 0
        for kr, _, s0, size in piece:
            s_scr[n % n_slots, row:row + size, :] = lax.dot_general(
                kr[0, h, s0:s0 + size, :], q_ref[0, h], (((1,), (1,)), ((), ())),
                preferred_element_type=F32)
            row += size

    def key_max(n):
        return jnp.max(s_scr[n % n_slots, :piece_rows(n), :], axis=0, keepdims=True)

    scores(0)
    if len(steps) > 1:
        scores(1)
    m_next = key_max(0)
    for n, (h, piece) in enumerate(steps):
        if n + 2 < len(steps):
            scores(n + 2)
        m_c = m_next
        if n + 1 < len(steps):
            m_next = key_max(n + 1)
        rows = piece_rows(n)
        s = s_scr[n % n_slots, :rows, :]
        first = n % len(pieces) == 0
        m_new = m_c if first else jnp.maximum(m, m_c)
        p = jnp.exp2(s - m_new)
        p_scr[n % 2, :rows, :] = p.astype(BF16)
        pv = None
        row = 0
        for _, vr, s0, size in piece:
            part = _dot(vr[0, h, :, s0:s0 + size], p_scr[n % 2, row:row + size, :])
            pv = part if pv is None else pv + part
            row += size
        if first:
            l = jnp.sum(p, axis=0, keepdims=True)
            acc = pv
        else:
            alpha = jnp.exp2(m - m_new)
            l = alpha * l + jnp.sum(p, axis=0, keepdims=True)
            acc = alpha * acc + pv
        m = m_new
        if (n + 1) % len(pieces) == 0:
            o_ref[0, :, h * V_DIM:(h + 1) * V_DIM] = (acc / l).T.astype(BF16)


def _attention(q, k, vt, cache, tq, heads, chunk):
    b, _, t, _ = q.shape
    tk = k.shape[2]
    in_specs = [
        pl.BlockSpec((1, heads, tq, HEAD_PAD), lambda i, h, j: (i, h, j, 0)),
        pl.BlockSpec((1, heads, tk, HEAD_PAD), lambda i, h, j: (i, h, 0, 0)),
        pl.BlockSpec((1, heads, V_DIM, tk), lambda i, h, j: (i, h, 0, 0)),
    ]
    args = [q, k, vt]
    piece_rows = min(chunk, tk)
    if cache is not None:
        past = cache[0].shape[2]
        piece_rows = max(piece_rows, (tk - 1) % chunk + 1 + past)
        in_specs += [
            pl.BlockSpec((1, heads, past, HEAD_PAD), lambda i, h, j: (i, h, 0, 0)),
            pl.BlockSpec((1, heads, V_DIM, past), lambda i, h, j: (i, h, 0, 0)),
        ]
        args += list(cache)
    return pl.pallas_call(
        functools.partial(_attn_kernel, heads, chunk, cache is not None),
        grid=(b, N_HEADS // heads, t // tq),
        in_specs=in_specs,
        out_specs=pl.BlockSpec((1, tq, heads * V_DIM), lambda i, h, j: (i, j, h)),
        out_shape=jax.ShapeDtypeStruct((b, t, N_HEADS * V_DIM), BF16),
        scratch_shapes=[
            pltpu.VMEM((3, piece_rows, tq), F32),
            pltpu.VMEM((2, piece_rows, tq), BF16),
        ],
        compiler_params=pltpu.CompilerParams(
            dimension_semantics=("arbitrary", "arbitrary", "arbitrary"),
            vmem_limit_bytes=VMEM_LIMIT),
        name="attention",
    )(*args)


def _conv_kernel(tm, xp_ref, x_ref, xn_ref, mod_ref, gmix_ref, w1_ref, b1_ref, wdw_ref, bdw_ref,
                 lng_ref, lnb_ref, w2_ref, b2_ref, o_ref, gflat_ref, gt_ref, yflat_ref):
    j = pl.program_id(1)
    rows = tm + 2 * CONV_HALO
    m = mod_ref[0]
    x = x_ref[0]
    x_ext = jnp.concatenate([xp_ref[0], x, xn_ref[0]], axis=0)
    h = _mod_norm(x_ext, gmix_ref[...], m[1:2], m[0:1]).astype(BF16)
    a = _dot(h, w1_ref[...]) + b1_ref[...]
    g = a[:, :D_MODEL] / (1.0 + jnp.exp(-a[:, D_MODEL:]))
    r = lax.broadcasted_iota(jnp.int32, (rows, 1), 0)
    valid = jnp.logical_and(jnp.logical_or(r >= CONV_HALO, j > 0),
                            jnp.logical_or(r < tm + CONV_HALO, j < pl.num_programs(1) - 1))
    g = jnp.where(valid, g, 0.0)

    n_tiles = D_MODEL // 128
    gs = rows + 8
    ys = tm + 8
    for c in range(n_tiles):
        gflat_ref[c * gs:c * gs + rows, :] = g[:, c * 128:(c + 1) * 128]

    def to_time_major(r, carry):
        gt_ref[r] = gflat_ref[pl.ds(r, n_tiles, stride=gs), :]
        return carry
    lax.fori_loop(0, rows, to_time_major, 0, unroll=8)

    first = CONV_HALO - CONV_PAD
    tb = 16
    bias = bdw_ref[...]

    def conv_block(i, carry):
        t0 = pl.multiple_of(i * tb, tb)
        accs = [bias] * tb
        for k in range(CONV_WIDTH):
            wk = wdw_ref[k]
            for u in range(tb):
                accs[u] = accs[u] + gt_ref[t0 + (u + first + k)] * wk
        for u in range(tb):
            yflat_ref[pl.ds(t0 + u, n_tiles, stride=ys), :] = accs[u]
        return carry
    lax.fori_loop(0, tm // tb, conv_block, 0)

    y = jnp.concatenate([yflat_ref[c * ys:c * ys + tm, :] for c in range(n_tiles)], axis=-1)
    mu = jnp.mean(y, axis=-1, keepdims=True)
    yc = y - mu
    var = jnp.mean(yc * yc, axis=-1, keepdims=True)
    z = _silu(yc * lax.rsqrt(var + EPS) * lng_ref[...] + lnb_ref[...]).astype(BF16)
    o_ref[0] = x + m[2:3] * (_dot(z, w2_ref[...]) + b2_ref[...])


def _conv_module(x, mod, cond_of_batch, gmix, w, tm):
    b, t, _ = x.shape
    nh = tm // CONV_HALO
    last = t // CONV_HALO - 1
    names = ("w1", "b1", "wdw", "bdw", "lng", "lnb", "w2", "b2")
    return pl.pallas_call(
        functools.partial(_conv_kernel, tm),
        grid=(b, t // tm),
        in_specs=[
            pl.BlockSpec((1, CONV_HALO, D_MODEL), lambda i, j: (i, jnp.maximum(j * nh - 1, 0), 0)),
            pl.BlockSpec((1, tm, D_MODEL), lambda i, j: (i, j, 0)),
            pl.BlockSpec((1, CONV_HALO, D_MODEL), lambda i, j: (i, jnp.minimum((j + 1) * nh, last), 0)),
            pl.BlockSpec((1, N_MOD, D_MODEL), lambda i, j: (cond_of_batch(i), 0, 0)),
            _const_spec((1, D_MODEL)),
        ] + [_const_spec(w[n].shape) for n in names],
        out_specs=pl.BlockSpec((1, tm, D_MODEL), lambda i, j: (i, j, 0)),
        out_shape=jax.ShapeDtypeStruct(x.shape, F32),
        scratch_shapes=[
            pltpu.VMEM((D_MODEL // 128 * (tm + 2 * CONV_HALO + 8), 128), F32),
            pltpu.VMEM((tm + 2 * CONV_HALO, D_MODEL // 128, 128), F32),
            pltpu.VMEM((D_MODEL // 128 * (tm + 8), 128), F32),
        ],
        compiler_params=pltpu.CompilerParams(
            dimension_semantics=("arbitrary", "arbitrary"), vmem_limit_bytes=VMEM_LIMIT),
        name="conv_module",
    )(x, x, x, mod, gmix, *[w[n] for n in names])


def _ffn_kernel(has_o, *refs):
    if has_o:
        x_ref, o_in_ref, wo_ref, mod_ref, gffn_ref, wg_ref, wu_ref, wd_ref, out_ref = refs
    else:
        x_ref, mod_ref, gffn_ref, wg_ref, wu_ref, wd_ref, out_ref = refs
    m = mod_ref[0]
    x = x_ref[...]
    if has_o:
        x = x + m[2:3] * _dot(o_in_ref[...], wo_ref[...])
    h = _mod_norm(x, gffn_ref[...], m[4:5], m[3:4]).astype(BF16)
    acc = jnp.zeros(x.shape, F32)
    for c in range(D_FF // FF_CHUNK):
        cs = slice(c * FF_CHUNK, (c + 1) * FF_CHUNK)
        a = _dot(h, wg_ref[:, cs])
        u = _dot(h, wu_ref[:, cs])
        acc = acc + _dot((_silu(a) * u).astype(BF16), wd_ref[cs, :])
    out_ref[...] = x + m[5:6] * acc


def _ffn(x, o, wo, mod, cond_of_block, gffn, w, tm):
    t = x.shape[0]
    tok_spec = pl.BlockSpec((tm, D_MODEL), lambda i: (i, 0))
    single = pl.Buffered(1)
    in_specs = [tok_spec]
    args = [x]
    if o is not None:
        in_specs += [tok_spec, pl.BlockSpec(wo.shape, lambda i: (0, 0), pipeline_mode=single)]
        args += [o, wo]
    in_specs += [
        pl.BlockSpec((1, N_MOD, D_MODEL), lambda i: (cond_of_block(i), 0, 0)),
        _const_spec((1, D_MODEL)),
        pl.BlockSpec((D_MODEL, D_FF), lambda i: (0, 0), pipeline_mode=single),
        pl.BlockSpec((D_MODEL, D_FF), lambda i: (0, 0), pipeline_mode=single),
        pl.BlockSpec((D_FF, D_MODEL), lambda i: (0, 0), pipeline_mode=single),
    ]
    args += [mod, gffn, w["wg"], w["wu"], w["wd"]]
    return pl.pallas_call(
        functools.partial(_ffn_kernel, o is not None),
        grid=(t // tm,),
        in_specs=in_specs,
        out_specs=tok_spec,
        out_shape=jax.ShapeDtypeStruct(x.shape, F32),
        compiler_params=pltpu.CompilerParams(
            dimension_semantics=("arbitrary",), vmem_limit_bytes=VMEM_LIMIT),
        name="ffn",
    )(*args)


def _rope_tables(t):
    rows = t // GRID_W
    tt = jnp.arange(t)
    row = jnp.repeat(jnp.arange(rows), GRID_W).astype(F32)
    col = (tt % GRID_W).astype(F32)
    ax = QK_ROPE // 2
    inv = 1.0 / (ROPE_THETA ** (jnp.arange(0, ax, 2, dtype=F32) / ax))
    ar = row[:, None] * inv[None, :]
    ac = col[:, None] * inv[None, :]
    cos = jnp.concatenate([jnp.cos(ar), jnp.cos(ar), jnp.cos(ac), jnp.cos(ac)], axis=-1)
    sin = jnp.concatenate([jnp.sin(ar), jnp.sin(ar), jnp.sin(ac), jnp.sin(ac)], axis=-1)
    q = QK_ROPE // 4
    lane = jnp.arange(QK_ROPE)
    first_half = (lane // q) % 2 == 0
    sin_a = jnp.where(first_half, -sin, 0.0)
    sin_b = jnp.where(first_half, 0.0, sin)
    pad = ((0, 0), (0, PE_PAD - QK_ROPE))
    return jnp.pad(cos, pad), jnp.pad(sin_a, pad), jnp.pad(sin_b, pad)


def _row(v):
    return v.reshape(1, -1)


def kernel(x_prompt, x_sample, cache_ckv, cache_kpe, c, c_ctx, w_mod, b_mod, norm_mix, norm_ffn, mla_w_dq, mla_q_norm, mla_w_uq, mla_w_dkv, mla_kv_norm, mla_w_ukv, mla_q_hnorm, mla_k_hnorm, mla_w_o, conv_w_pw1, conv_b_pw1, conv_w_dw, conv_b_dw, conv_ln_g, conv_ln_b, conv_w_pw2, conv_b_pw2, ffn_w_gate, ffn_w_up, ffn_w_down):
    bp, tp, _ = x_prompt.shape
    bs, ts, _ = x_sample.shape

    cond = jnp.zeros((COND_ROWS, D_MODEL), F32).at[0].set(c_ctx).at[1:1 + bs].set(c)
    mod_all = _modulation(cond, w_mod, b_mod).reshape(DEPTH, COND_ROWS, N_MOD, D_MODEL)

    rope = _rope_tables(ts)
    cache_kpe_pad = jnp.pad(cache_kpe, ((0, 0), (0, 0), (0, 0), (0, PE_PAD - QK_ROPE)))

    ffn_tm = 512
    blocks_per_sample = ts // ffn_tm
    cond_prompt = lambda i: 0
    cond_sample_blk = lambda i: 1 + i // blocks_per_sample
    cond_sample_batch = lambda i: 1 + i

    xp = x_prompt
    xs = x_sample
    new_ckv, new_kpe = [], []
    for l in range(DEPTH):
        mod = mod_all[l]
        gmix = _row(norm_mix[l])
        gffn = _row(norm_ffn[l])
        fw = {"wg": ffn_w_gate[l].astype(BF16), "wu": ffn_w_up[l].astype(BF16),
              "wd": ffn_w_down[l].astype(BF16)}
        j = l // 2
        if l % 2 == 0:
            wuq = mla_w_uq[j].reshape(Q_LORA, N_HEADS, QK_DIM)
            wuq = jnp.pad(wuq, ((0, 0), (0, 0), (0, HEAD_PAD - QK_DIM)))
            wukv = mla_w_ukv[j].reshape(KV_LORA, N_HEADS, QK_NOPE + V_DIM)
            mw = {
                "wdq": mla_w_dq[j].astype(BF16),
                "qn": _row(mla_q_norm[j]),
                "wuq": wuq.reshape(Q_LORA, N_HEADS * HEAD_PAD).astype(BF16),
                "wdkv": jnp.pad(mla_w_dkv[j], ((0, 0), (0, PE_PAD - QK_ROPE))).astype(BF16),
                "kvn": _row(mla_kv_norm[j]),
                "wk": wukv[:, :, :QK_NOPE].reshape(KV_LORA, N_HEADS * QK_NOPE).astype(BF16),
                "wvt": wukv[:, :, QK_NOPE:].reshape(KV_LORA, N_HEADS * V_DIM).T.astype(BF16),
                "qhn": _row(jnp.pad(mla_q_hnorm[j], (0, HEAD_PAD - QK_DIM))),
                "khn": _row(jnp.pad(mla_k_hnorm[j], (0, HEAD_PAD - QK_DIM))),
            }
            wo = mla_w_o[j].astype(BF16)
            qp, kp, vtp, ckv_p, kpe_p = _mla_proj(xp, mod, cond_prompt, gmix, mw, tp, None)
            new_ckv.append(ckv_p)
            new_kpe.append(kpe_p)
            op = _attention(qp, kp, vtp, None, tp, N_HEADS, tp)
            cache = _cache_expand(cache_ckv, cache_kpe_pad, j, mw)
            qs, ks, vts = _mla_proj(xs, mod, cond_sample_batch, gmix, mw, 512, rope)
            os_ = _attention(qs, ks, vts, cache, 512, 4, 1024)
            xp2 = _ffn(xp.reshape(bp * tp, D_MODEL), op.reshape(bp * tp, D_MODEL), wo, mod,
                       cond_prompt, gffn, fw, ffn_tm)
            xs2 = _ffn(xs.reshape(bs * ts, D_MODEL), os_.reshape(bs * ts, D_MODEL), wo, mod,
                       cond_sample_blk, gffn, fw, ffn_tm)
        else:
            cw = {
                "w1": conv_w_pw1[j].astype(BF16), "b1": _row(conv_b_pw1[j]),
                "wdw": conv_w_dw[j].reshape(CONV_WIDTH, D_MODEL // 128, 128),
                "bdw": conv_b_dw[j].reshape(D_MODEL // 128, 128),
                "lng": _row(conv_ln_g[j]), "lnb": _row(conv_ln_b[j]),
                "w2": conv_w_pw2[j].astype(BF16), "b2": _row(conv_b_pw2[j]),
            }
            xp1 = _conv_module(xp, mod, cond_prompt, gmix, cw, tp)
            xs1 = _conv_module(xs, mod, cond_sample_batch, gmix, cw, 512)
            xp2 = _ffn(xp1.reshape(bp * tp, D_MODEL), None, None, mod, cond_prompt, gffn, fw, ffn_tm)
            xs2 = _ffn(xs1.reshape(bs * ts, D_MODEL), None, None, mod, cond_sample_blk, gffn, fw,
                       ffn_tm)
        xp = xp2.reshape(bp, tp, D_MODEL)
        xs = xs2.reshape(bs, ts, D_MODEL)

    state_ckv = jnp.stack(new_ckv, axis=1)
    state_kpe = jnp.stack(new_kpe, axis=1)
    return (xp, xs, state_ckv, state_kpe)
```

```python
import functools

import jax
import jax.numpy as jnp
import numpy as np
from jax import lax
from jax.experimental import pallas as pl
from jax.experimental.pallas import tpu as pltpu

F32 = jnp.float32
BF16 = jnp.bfloat16

D_MODEL = 1024
DEPTH = 4
N_MOD = 6
GRID_W = 64
N_HEADS = 8
Q_LORA = 512
KV_LORA = 256
QK_NOPE = 128
QK_ROPE = 64
QK_DIM = QK_NOPE + QK_ROPE
V_DIM = 128
HEAD_PAD = 256
PE_PAD = HEAD_PAD - QK_NOPE
ROPE_THETA = 10000.0
CONV_WIDTH = 31
CONV_PAD = (CONV_WIDTH - 1) // 2
CONV_HALO = 16
CONV_CHUNKS = 2
D_FF = 2816
FF_CHUNK = 1408
EPS = 1e-6
LOG2_E = 1.4426950408889634
COND_ROWS = 8

VMEM_LIMIT = 56 * 1024 * 1024


def _dot(a, b):
    return jnp.dot(a, b, preferred_element_type=F32)


def _silu(a):
    return a / (1.0 + jnp.exp(-a))


def _rms(x, n):
    return x * lax.rsqrt(jnp.sum(x * x, axis=-1, keepdims=True) * (1.0 / n) + EPS)


def _mod_norm(x, gain, scale, shift):
    return _rms(x, x.shape[-1]) * (gain * (1.0 + scale)) + shift


def _split_bf16(a):
    hi = a.astype(BF16)
    lo = (a - hi.astype(F32)).astype(BF16)
    return hi, lo


def _mod_kernel(c_ref, w_ref, b_ref, o_ref):
    s_hi, s_lo = _split_bf16(_silu(c_ref[...]))
    w_hi, w_lo = _split_bf16(w_ref[0])
    o_ref[0] = _dot(s_hi, w_hi) + _dot(s_lo, w_hi) + _dot(s_hi, w_lo) + b_ref[0]


def _modulation(cond, w_mod, b_mod):
    tn = 1536
    n_out = N_MOD * D_MODEL
    return pl.pallas_call(
        _mod_kernel,
        grid=(DEPTH, n_out // tn),
        in_specs=[
            pl.BlockSpec((COND_ROWS, D_MODEL), lambda l, n: (0, 0)),
            pl.BlockSpec((1, D_MODEL, tn), lambda l, n: (l, 0, n)),
            pl.BlockSpec((1, 1, tn), lambda l, n: (l, 0, n)),
        ],
        out_specs=pl.BlockSpec((1, COND_ROWS, tn), lambda l, n: (l, 0, n)),
        out_shape=jax.ShapeDtypeStruct((DEPTH, COND_ROWS, n_out), F32),
        compiler_params=pltpu.CompilerParams(
            dimension_semantics=("arbitrary", "arbitrary"), vmem_limit_bytes=VMEM_LIMIT),
        name="modulation",
    )(cond, w_mod, b_mod.reshape(DEPTH, 1, n_out))


def _expand_kv(ckv, kpe_rot, kpe_ss, wk_ref, wvt_ref, khn_ref, k_ref, vt_ref, rows):
    ckv_b = ckv.astype(BF16)
    kn_all = _dot(ckv_b, wk_ref[...])
    vt_all = lax.dot_general(wvt_ref[...], ckv_b, (((1,), (1,)), ((), ())),
                             preferred_element_type=F32)
    gain_nope = khn_ref[:, :QK_NOPE]
    for h in range(N_HEADS):
        kn = kn_all[:, h * QK_NOPE:(h + 1) * QK_NOPE]
        ss = jnp.sum(kn * kn, axis=-1, keepdims=True) + kpe_ss
        inv = lax.rsqrt(ss * (1.0 / QK_DIM) + EPS)
        k_ref[0, h, rows, :QK_NOPE] = (kn * inv * gain_nope).astype(BF16)
        k_ref[0, h, rows, QK_NOPE:] = (kpe_rot * inv).astype(BF16)
        vt_ref[0, h, :, rows] = vt_all[h * V_DIM:(h + 1) * V_DIM, :].astype(BF16)


def _rope(pe, cos, sin_a, sin_b):
    return (pe * cos + pltpu.roll(pe, PE_PAD - QK_ROPE // 4, axis=1) * sin_a
            + pltpu.roll(pe, QK_ROPE // 4, axis=1) * sin_b)


_MLA_WEIGHTS = ("wdq", "qn", "wuq", "wdkv", "kvn", "wk", "wvt", "qhn", "khn")


def _mla_proj_kernel(use_rope, emit_state, sub, *refs):
    refs = list(refs)
    n_in = 3 + len(_MLA_WEIGHTS)
    (x_ref, mod_ref, gmix_ref, wdq_ref, qn_ref, wuq_ref, wdkv_ref, kvn_ref, wk_ref, wvt_ref,
     qhn_ref, khn_ref) = refs[:n_in]
    refs = refs[n_in:]
    if use_rope:
        cos_ref, sina_ref, sinb_ref = refs[:3]
        refs = refs[3:]
    q_ref, k_ref, vt_ref = refs[:3]
    refs = refs[3:]

    m = mod_ref[0]
    gain = gmix_ref[...] * (1.0 + m[1:2])
    q_gain = qhn_ref[...] * (QK_DIM ** -0.5 * LOG2_E)
    tm = x_ref.shape[1]
    for r0 in range(0, tm, sub):
        rows = slice(r0, r0 + sub)
        h = (_rms(x_ref[0, rows, :], D_MODEL) * gain + m[0:1]).astype(BF16)

        kv = _dot(h, wdkv_ref[...])
        ckv = _rms(kv[:, :KV_LORA], KV_LORA) * kvn_ref[...]
        kpe = kv[:, KV_LORA:]
        if emit_state:
            ckv_ref, kpe_ref = refs
            ckv_ref[0, rows, :] = ckv
            kpe_ref[0, rows, :] = kpe[:, :QK_ROPE]
        kpe_ss = jnp.sum(kpe * kpe, axis=-1, keepdims=True)
        kpe_rot = kpe * khn_ref[:, QK_NOPE:]
        if use_rope:
            cos, sin_a, sin_b = cos_ref[rows, :], sina_ref[rows, :], sinb_ref[rows, :]
            kpe_rot = _rope(kpe_rot, cos, sin_a, sin_b)
        _expand_kv(ckv, kpe_rot, kpe_ss, wk_ref, wvt_ref, khn_ref, k_ref, vt_ref, rows)

        cq = (_rms(_dot(h, wdq_ref[...]), Q_LORA) * qn_ref[...]).astype(BF16)
        qf = _dot(cq, wuq_ref[...])
        for hd in range(N_HEADS):
            qh = qf[:, hd * HEAD_PAD:(hd + 1) * HEAD_PAD]
            inv = lax.rsqrt(jnp.sum(qh * qh, axis=-1, keepdims=True) * (1.0 / QK_DIM) + EPS)
            qn = qh * inv * q_gain
            q_ref[0, hd, rows, :QK_NOPE] = qn[:, :QK_NOPE].astype(BF16)
            qpe = qn[:, QK_NOPE:]
            if use_rope:
                qpe = _rope(qpe, cos, sin_a, sin_b)
            q_ref[0, hd, rows, QK_NOPE:] = qpe.astype(BF16)


def _cache_expand_kernel(ckv_ref, kpe_ref, wk_ref, wvt_ref, khn_ref, k_ref, vt_ref):
    kpe = kpe_ref[0]
    kpe_ss = jnp.sum(kpe * kpe, axis=-1, keepdims=True)
    _expand_kv(ckv_ref[0], kpe * khn_ref[:, QK_NOPE:], kpe_ss, wk_ref, wvt_ref, khn_ref,
               k_ref, vt_ref, slice(0, kpe.shape[0]))


def _const_spec(shape):
    nd = len(shape)
    return pl.BlockSpec(shape, lambda *_: (0,) * nd)


def _mla_proj(x, mod, cond_of_batch, gmix, w, tm, rope):
    b, t, _ = x.shape
    use_rope = rope is not None
    kern = functools.partial(_mla_proj_kernel, use_rope, not use_rope, min(tm, 256))
    in_specs = [
        pl.BlockSpec((1, tm, D_MODEL), lambda i, j: (i, j, 0)),
        pl.BlockSpec((1, N_MOD, D_MODEL), lambda i, j: (cond_of_batch(i), 0, 0)),
        _const_spec((1, D_MODEL)),
    ] + [_const_spec(w[n].shape) for n in _MLA_WEIGHTS]
    args = [x, mod, gmix] + [w[n] for n in _MLA_WEIGHTS]
    out_specs = [
        pl.BlockSpec((1, N_HEADS, tm, HEAD_PAD), lambda i, j: (i, 0, j, 0)),
        pl.BlockSpec((1, N_HEADS, tm, HEAD_PAD), lambda i, j: (i, 0, j, 0)),
        pl.BlockSpec((1, N_HEADS, V_DIM, tm), lambda i, j: (i, 0, 0, j)),
    ]
    out_shape = [
        jax.ShapeDtypeStruct((b, N_HEADS, t, HEAD_PAD), BF16),
        jax.ShapeDtypeStruct((b, N_HEADS, t, HEAD_PAD), BF16),
        jax.ShapeDtypeStruct((b, N_HEADS, V_DIM, t), BF16),
    ]
    if use_rope:
        in_specs += [pl.BlockSpec((tm, PE_PAD), lambda i, j: (j, 0))] * 3
        args += list(rope)
    else:
        out_specs += [pl.BlockSpec((1, tm, KV_LORA), lambda i, j: (i, j, 0)),
                      pl.BlockSpec((1, tm, QK_ROPE), lambda i, j: (i, j, 0))]
        out_shape += [jax.ShapeDtypeStruct((b, t, KV_LORA), F32),
                      jax.ShapeDtypeStruct((b, t, QK_ROPE), F32)]
    return pl.pallas_call(
        kern,
        grid=(b, t // tm),
        in_specs=in_specs,
        out_specs=out_specs,
        out_shape=out_shape,
        compiler_params=pltpu.CompilerParams(
            dimension_semantics=("arbitrary", "arbitrary"), vmem_limit_bytes=VMEM_LIMIT),
        name="mla_proj",
    )(*args)


def _cache_expand(cache_ckv, cache_kpe_pad, j, w):
    b, _, past, _ = cache_ckv.shape
    names = ("wk", "wvt", "khn")
    return pl.pallas_call(
        _cache_expand_kernel,
        grid=(b,),
        in_specs=[
            pl.BlockSpec((1, None, past, KV_LORA), lambda i: (i, j, 0, 0)),
            pl.BlockSpec((1, None, past, PE_PAD), lambda i: (i, j, 0, 0)),
        ] + [_const_spec(w[n].shape) for n in names],
        out_specs=[
            pl.BlockSpec((1, N_HEADS, past, HEAD_PAD), lambda i: (i, 0, 0, 0)),
            pl.BlockSpec((1, N_HEADS, V_DIM, past), lambda i: (i, 0, 0, 0)),
        ],
        out_shape=[
            jax.ShapeDtypeStruct((b, N_HEADS, past, HEAD_PAD), BF16),
            jax.ShapeDtypeStruct((b, N_HEADS, V_DIM, past), BF16),
        ],
        compiler_params=pltpu.CompilerParams(
            dimension_semantics=("arbitrary",), vmem_limit_bytes=VMEM_LIMIT),
        name="cache_expand",
    )(cache_ckv, cache_kpe_pad, *[w[n] for n in names])


def _attn_kernel(heads, chunk, has_cache, *refs):
    if has_cache:
        q_ref, k_ref, vt_ref, kc_ref, vtc_ref, o_ref, s_scr, p_scr = refs
    else:
        q_ref, k_ref, vt_ref, o_ref, s_scr, p_scr = refs
    tk = k_ref.shape[2]
    pieces = [[(k_ref, vt_ref, s0, min(chunk, tk - s0))] for s0 in range(0, tk, chunk)]
    if has_cache:
        pieces[-1].append((kc_ref, vtc_ref, 0, kc_ref.shape[2]))
    steps = [(h, piece) for h in range(heads) for piece in pieces]

    n_slots = s_scr.shape[0]

    def piece_rows(n):
        return sum(size for _, _, _, size in steps[n][1])

    def scores(n):
        h, piece = steps[n]
        row = 0
        for kr, _, s0, size in piece:
            s_scr[n % n_slots, row:row + size, :] = lax.dot_general(
                kr[0, h, s0:s0 + size, :], q_ref[0, h], (((1,), (1,)), ((), ())),
                preferred_element_type=F32)
            row += size

    def key_max(n):
        return jnp.max(s_scr[n % n_slots, :piece_rows(n), :], axis=0, keepdims=True)

    scores(0)
    if len(steps) > 1:
        scores(1)
    m_next = key_max(0)
    for n, (h, piece) in enumerate(steps):
        if n + 2 < len(steps):
            scores(n + 2)
        m_c = m_next
        if n + 1 < len(steps):
            m_next = key_max(n + 1)
        rows = piece_rows(n)
        s = s_scr[n % n_slots, :rows, :]
        first = n % len(pieces) == 0
        m_new = m_c if first else jnp.maximum(m, m_c)
        p = jnp.exp2(s - m_new)
        p_scr[n % 2, :rows, :] = p.astype(BF16)
        pv = None
        row = 0
        for _, vr, s0, size in piece:
            part = _dot(vr[0, h, :, s0:s0 + size], p_scr[n % 2, row:row + size, :])
            pv = part if pv is None else pv + part
            row += size
        if first:
            l = jnp.sum(p, axis=0, keepdims=True)
            acc = pv
        else:
            alpha = jnp.exp2(m - m_new)
            l = alpha * l + jnp.sum(p, axis=0, keepdims=True)
            acc = alpha * acc + pv
        m = m_new
        if (n + 1) % len(pieces) == 0:
            o_ref[0, :, h * V_DIM:(h + 1) * V_DIM] = (acc / l).T.astype(BF16)


def _attention(q, k, vt, cache, tq, heads, chunk):
    b, _, t, _ = q.shape
    tk = k.shape[2]
    in_specs = [
        pl.BlockSpec((1, heads, tq, HEAD_PAD), lambda i, h, j: (i, h, j, 0)),
        pl.BlockSpec((1, heads, tk, HEAD_PAD), lambda i, h, j: (i, h, 0, 0)),
        pl.BlockSpec((1, heads, V_DIM, tk), lambda i, h, j: (i, h, 0, 0)),
    ]
    args = [q, k, vt]
    piece_rows = min(chunk, tk)
    if cache is not None:
        past = cache[0].shape[2]
        piece_rows = max(piece_rows, (tk - 1) % chunk + 1 + past)
        in_specs += [
            pl.BlockSpec((1, heads, past, HEAD_PAD), lambda i, h, j: (i, h, 0, 0)),
            pl.BlockSpec((1, heads, V_DIM, past), lambda i, h, j: (i, h, 0, 0)),
        ]
        args += list(cache)
    return pl.pallas_call(
        functools.partial(_attn_kernel, heads, chunk, cache is not None),
        grid=(b, N_HEADS // heads, t // tq),
        in_specs=in_specs,
        out_specs=pl.BlockSpec((1, tq, heads * V_DIM), lambda i, h, j: (i, j, h)),
        out_shape=jax.ShapeDtypeStruct((b, t, N_HEADS * V_DIM), BF16),
        scratch_shapes=[
            pltpu.VMEM((3, piece_rows, tq), F32),
            pltpu.VMEM((2, piece_rows, tq), BF16),
        ],
        compiler_params=pltpu.CompilerParams(
            dimension_semantics=("arbitrary", "arbitrary", "arbitrary"),
            vmem_limit_bytes=VMEM_LIMIT),
        name="attention",
    )(*args)


def _conv_kernel(tm, xp_ref, x_ref, xn_ref, mod_ref, gmix_ref, w1_ref, b1_ref, wdw_ref, bdw_ref,
                 lng_ref, lnb_ref, w2_ref, b2_ref, o_ref, gflat_ref, gt_ref, yflat_ref):
    j = pl.program_id(1)
    has_prev = j > 0
    has_next = j < pl.num_programs(1) - 1
    rows = tm + 2 * CONV_HALO
    m = mod_ref[0]
    gain = gmix_ref[...] * (1.0 + m[1:2])
    shift, gate = m[0:1], m[2:3]
    n_tiles = D_MODEL // 128
    gs = rows + 8
    ys = tm + 8
    first = CONV_HALO - CONV_PAD
    bias = bdw_ref[...]
    n_chunks = CONV_CHUNKS
    chunk = tm // n_chunks

    def stage_a(c):
        x_c = x_ref[0, c * chunk:(c + 1) * chunk, :]
        e0 = c * chunk + CONV_HALO
        parts = [x_c]
        if c == 0:
            parts, e0 = [xp_ref[0]] + parts, 0
        if c == n_chunks - 1:
            parts = parts + [xn_ref[0]]
        x_e = jnp.concatenate(parts, axis=0) if len(parts) > 1 else x_c
        size = x_e.shape[0]
        h = (_rms(x_e, D_MODEL) * gain + shift).astype(BF16)
        a = _dot(h, w1_ref[...]) + b1_ref[...]
        g = a[:, :D_MODEL] / (1.0 + jnp.exp(-a[:, D_MODEL:]))
        if c == 0 or c == n_chunks - 1:
            r = lax.broadcasted_iota(jnp.int32, (size, 1), 0) + e0
            valid = jnp.logical_and(jnp.logical_or(r >= CONV_HALO, has_prev),
                                    jnp.logical_or(r < tm + CONV_HALO, has_next))
            g = jnp.where(valid, g, 0.0)
        for t in range(n_tiles):
            gflat_ref[t * gs + e0:t * gs + e0 + size, :] = g[:, t * 128:(t + 1) * 128]
        for e in range(e0, e0 + size):
            gt_ref[e] = gflat_ref[pl.ds(e, n_tiles, stride=gs), :]

    def stage_c(c):
        tb = 16
        for t0 in range(c * chunk, (c + 1) * chunk, tb):
            accs = [bias] * tb
            for k in range(CONV_WIDTH):
                wk = wdw_ref[k]
                for u in range(tb):
                    accs[u] = accs[u] + gt_ref[t0 + u + first + k] * wk
            for u in range(tb):
                yflat_ref[pl.ds(t0 + u, n_tiles, stride=ys), :] = accs[u]

    def stage_d(c):
        r0 = c * chunk
        y = jnp.concatenate([yflat_ref[t * ys + r0:t * ys + r0 + chunk, :]
                             for t in range(n_tiles)], axis=-1)
        mu = jnp.mean(y, axis=-1, keepdims=True)
        yc = y - mu
        var = jnp.mean(yc * yc, axis=-1, keepdims=True)
        z = _silu(yc * lax.rsqrt(var + EPS) * lng_ref[...] + lnb_ref[...]).astype(BF16)
        o_ref[0, r0:r0 + chunk, :] = (x_ref[0, r0:r0 + chunk, :]
                                           + gate * (_dot(z, w2_ref[...]) + b2_ref[...]))

    stage_a(0)
    for c in range(n_chunks):
        if c + 1 < n_chunks:
            stage_a(c + 1)
        stage_c(c)
        stage_d(c)


def _conv_module(x, mod, cond_of_batch, gmix, w, tm):
    b, t, _ = x.shape
    nh = tm // CONV_HALO
    last = t // CONV_HALO - 1
    names = ("w1", "b1", "wdw", "bdw", "lng", "lnb", "w2", "b2")
    return pl.pallas_call(
        functools.partial(_conv_kernel, tm),
        grid=(b, t // tm),
        in_specs=[
            pl.BlockSpec((1, CONV_HALO, D_MODEL), lambda i, j: (i, jnp.maximum(j * nh - 1, 0), 0)),
            pl.BlockSpec((1, tm, D_MODEL), lambda i, j: (i, j, 0)),
            pl.BlockSpec((1, CONV_HALO, D_MODEL), lambda i, j: (i, jnp.minimum((j + 1) * nh, last), 0)),
            pl.BlockSpec((1, N_MOD, D_MODEL), lambda i, j: (cond_of_batch(i), 0, 0)),
            _const_spec((1, D_MODEL)),
        ] + [_const_spec(w[n].shape) for n in names],
        out_specs=pl.BlockSpec((1, tm, D_MODEL), lambda i, j: (i, j, 0)),
        out_shape=jax.ShapeDtypeStruct(x.shape, F32),
        scratch_shapes=[
            pltpu.VMEM((D_MODEL // 128 * (tm + 2 * CONV_HALO + 8), 128), F32),
            pltpu.VMEM((tm + 2 * CONV_HALO, D_MODEL // 128, 128), F32),
            pltpu.VMEM((D_MODEL // 128 * (tm + 8), 128), F32),
        ],
        compiler_params=pltpu.CompilerParams(
            dimension_semantics=("arbitrary", "arbitrary"), vmem_limit_bytes=VMEM_LIMIT),
        name="conv_module",
    )(x, x, x, mod, gmix, *[w[n] for n in names])


def _ffn_kernel(has_o, *refs):
    if has_o:
        x_ref, o_in_ref, wo_ref, mod_ref, gffn_ref, wg_ref, wu_ref, wd_ref, out_ref = refs
    else:
        x_ref, mod_ref, gffn_ref, wg_ref, wu_ref, wd_ref, out_ref = refs
    m = mod_ref[0]
    x = x_ref[...]
    if has_o:
        x = x + m[2:3] * _dot(o_in_ref[...], wo_ref[...])
    h = _mod_norm(x, gffn_ref[...], m[4:5], m[3:4]).astype(BF16)
    acc = jnp.zeros(x.shape, F32)
    for c in range(D_FF // FF_CHUNK):
        cs = slice(c * FF_CHUNK, (c + 1) * FF_CHUNK)
        a = _dot(h, wg_ref[:, cs])
        u = _dot(h, wu_ref[:, cs])
        acc = acc + _dot((_silu(a) * u).astype(BF16), wd_ref[cs, :])
    out_ref[...] = x + m[5:6] * acc


def _ffn(x, o, wo, mod, cond_of_block, gffn, w, tm):
    t = x.shape[0]
    tok_spec = pl.BlockSpec((tm, D_MODEL), lambda i: (i, 0))
    single = pl.Buffered(1)
    in_specs = [tok_spec]
    args = [x]
    if o is not None:
        in_specs += [tok_spec, pl.BlockSpec(wo.shape, lambda i: (0, 0), pipeline_mode=single)]
        args += [o, wo]
    in_specs += [
        pl.BlockSpec((1, N_MOD, D_MODEL), lambda i: (cond_of_block(i), 0, 0)),
        _const_spec((1, D_MODEL)),
        pl.BlockSpec((D_MODEL, D_FF), lambda i: (0, 0), pipeline_mode=single),
        pl.BlockSpec((D_MODEL, D_FF), lambda i: (0, 0), pipeline_mode=single),
        pl.BlockSpec((D_FF, D_MODEL), lambda i: (0, 0), pipeline_mode=single),
    ]
    args += [mod, gffn, w["wg"], w["wu"], w["wd"]]
    return pl.pallas_call(
        functools.partial(_ffn_kernel, o is not None),
        grid=(t // tm,),
        in_specs=in_specs,
        out_specs=tok_spec,
        out_shape=jax.ShapeDtypeStruct(x.shape, F32),
        compiler_params=pltpu.CompilerParams(
            dimension_semantics=("arbitrary",), vmem_limit_bytes=VMEM_LIMIT),
        name="ffn",
    )(*args)


def _rope_tables(t):
    f32 = np.float32
    rows = t // GRID_W
    row = np.repeat(np.arange(rows), GRID_W).astype(f32)
    col = (np.arange(t) % GRID_W).astype(f32)
    ax = QK_ROPE // 2
    inv = (f32(1.0) / (f32(ROPE_THETA) ** (np.arange(0, ax, 2, dtype=f32) / f32(ax)))).astype(f32)
    ar = row[:, None] * inv[None, :]
    ac = col[:, None] * inv[None, :]
    cos = np.concatenate([np.cos(ar), np.cos(ar), np.cos(ac), np.cos(ac)], axis=-1)
    sin = np.concatenate([np.sin(ar), np.sin(ar), np.sin(ac), np.sin(ac)], axis=-1)
    q = QK_ROPE // 4
    first_half = (np.arange(QK_ROPE) // q) % 2 == 0
    sin_a = np.where(first_half, -sin, f32(0.0))
    sin_b = np.where(first_half, f32(0.0), sin)
    pad = ((0, 0), (0, PE_PAD - QK_ROPE))
    return tuple(jnp.asarray(np.pad(a, pad).astype(f32)) for a in (cos, sin_a, sin_b))


def _row(v):
    return v.reshape(1, -1)


def kernel(x_prompt, x_sample, cache_ckv, cache_kpe, c, c_ctx, w_mod, b_mod, norm_mix, norm_ffn, mla_w_dq, mla_q_norm, mla_w_uq, mla_w_dkv, mla_kv_norm, mla_w_ukv, mla_q_hnorm, mla_k_hnorm, mla_w_o, conv_w_pw1, conv_b_pw1, conv_w_dw, conv_b_dw, conv_ln_g, conv_ln_b, conv_w_pw2, conv_b_pw2, ffn_w_gate, ffn_w_up, ffn_w_down):
    bp, tp, _ = x_prompt.shape
    bs, ts, _ = x_sample.shape

    cond = jnp.zeros((COND_ROWS, D_MODEL), F32).at[0].set(c_ctx).at[1:1 + bs].set(c)
    mod_all = _modulation(cond, w_mod, b_mod).reshape(DEPTH, COND_ROWS, N_MOD, D_MODEL)

    rope = _rope_tables(ts)
    cache_kpe_pad = jnp.pad(cache_kpe, ((0, 0), (0, 0), (0, 0), (0, PE_PAD - QK_ROPE)))

    ffn_tm = 512
    blocks_per_sample = ts // ffn_tm
    cond_prompt = lambda i: 0
    cond_sample_blk = lambda i: 1 + i // blocks_per_sample
    cond_sample_batch = lambda i: 1 + i

    xp = x_prompt
    xs = x_sample
    new_ckv, new_kpe = [], []
    for l in range(DEPTH):
        mod = mod_all[l]
        gmix = _row(norm_mix[l])
        gffn = _row(norm_ffn[l])
        fw = {"wg": ffn_w_gate[l].astype(BF16), "wu": ffn_w_up[l].astype(BF16),
              "wd": ffn_w_down[l].astype(BF16)}
        j = l // 2
        if l % 2 == 0:
            wuq = mla_w_uq[j].reshape(Q_LORA, N_HEADS, QK_DIM)
            wuq = jnp.pad(wuq, ((0, 0), (0, 0), (0, HEAD_PAD - QK_DIM)))
            wukv = mla_w_ukv[j].reshape(KV_LORA, N_HEADS, QK_NOPE + V_DIM)
            mw = {
                "wdq": mla_w_dq[j].astype(BF16),
                "qn": _row(mla_q_norm[j]),
                "wuq": wuq.reshape(Q_LORA, N_HEADS * HEAD_PAD).astype(BF16),
                "wdkv": jnp.pad(mla_w_dkv[j], ((0, 0), (0, PE_PAD - QK_ROPE))).astype(BF16),
                "kvn": _row(mla_kv_norm[j]),
                "wk": wukv[:, :, :QK_NOPE].reshape(KV_LORA, N_HEADS * QK_NOPE).astype(BF16),
                "wvt": wukv[:, :, QK_NOPE:].reshape(KV_LORA, N_HEADS * V_DIM).T.astype(BF16),
                "qhn": _row(jnp.pad(mla_q_hnorm[j], (0, HEAD_PAD - QK_DIM))),
                "khn": _row(jnp.pad(mla_k_hnorm[j], (0, HEAD_PAD - QK_DIM))),
            }
            wo = mla_w_o[j].astype(BF16)
            qp, kp, vtp, ckv_p, kpe_p = _mla_proj(xp, mod, cond_prompt, gmix, mw, tp, None)
            new_ckv.append(ckv_p)
            new_kpe.append(kpe_p)
            op = _attention(qp, kp, vtp, None, tp, N_HEADS, tp)
            cache = _cache_expand(cache_ckv, cache_kpe_pad, j, mw)
            qs, ks, vts = _mla_proj(xs, mod, cond_sample_batch, gmix, mw, 512, rope)
            os_ = _attention(qs, ks, vts, cache, 512, 4, 1024)
            xp2 = _ffn(xp.reshape(bp * tp, D_MODEL), op.reshape(bp * tp, D_MODEL), wo, mod,
                       cond_prompt, gffn, fw, ffn_tm)
            xs2 = _ffn(xs.reshape(bs * ts, D_MODEL), os_.reshape(bs * ts, D_MODEL), wo, mod,
                       cond_sample_blk, gffn, fw, ffn_tm)
        else:
            cw = {
                "w1": conv_w_pw1[j].astype(BF16), "b1": _row(conv_b_pw1[j]),
                "wdw": conv_w_dw[j].reshape(CONV_WIDTH, D_MODEL // 128, 128),
                "bdw": conv_b_dw[j].reshape(D_MODEL // 128, 128),
                "lng": _row(conv_ln_g[j]), "lnb": _row(conv_ln_b[j]),
                "w2": conv_w_pw2[j].astype(BF16), "b2": _row(conv_b_pw2[j]),
            }
            xp1 = _conv_module(xp, mod, cond_prompt, gmix, cw, tp)
            xs1 = _conv_module(xs, mod, cond_sample_batch, gmix, cw, 512)
            xp2 = _ffn(xp1.reshape(bp * tp, D_MODEL), None, None, mod, cond_prompt, gffn, fw, ffn_tm)
            xs2 = _ffn(xs1.reshape(bs * ts, D_MODEL), None, None, mod, cond_sample_blk, gffn, fw,
                       ffn_tm)
        xp = xp2.reshape(bp, tp, D_MODEL)
        xs = xs2.reshape(bs, ts, D_MODEL)

    state_ckv = jnp.stack(new_ckv, axis=1)
    state_kpe = jnp.stack(new_kpe, axis=1)
    return (xp, xs, state_ckv, state_kpe)
```

```python
import functools

import jax
import jax.numpy as jnp
import numpy as np
from jax import lax
from jax.experimental import pallas as pl
from jax.experimental.pallas import tpu as pltpu

F32 = jnp.float32
BF16 = jnp.bfloat16

D_MODEL = 1024
DEPTH = 4
N_MOD = 6
GRID_W = 64
N_HEADS = 8
Q_LORA = 512
KV_LORA = 256
QK_NOPE = 128
QK_ROPE = 64
QK_DIM = QK_NOPE + QK_ROPE
V_DIM = 128
VT_ROWS = V_DIM + 16
HEAD_PAD = 256
PE_PAD = HEAD_PAD - QK_NOPE
ROPE_THETA = 10000.0
CONV_WIDTH = 31
CONV_PAD = (CONV_WIDTH - 1) // 2
CONV_HALO = 16
CONV_CHUNKS = 2
D_FF = 2816
MXU_TILE = 256
FF_SPLIT = 6 * MXU_TILE
EPS = 1e-6
LOG2_E = 1.4426950408889634
COND_ROWS = 8

VMEM_LIMIT = 56 * 1024 * 1024


def _dot(a, b):
    return jnp.dot(a, b, preferred_element_type=F32)


def _silu(a):
    return a / (1.0 + jnp.exp(-a))


def _rms(x, n):
    return x * lax.rsqrt(jnp.sum(x * x, axis=-1, keepdims=True) * (1.0 / n) + EPS)


def _mod_norm(x, gain, scale, shift):
    return _rms(x, x.shape[-1]) * (gain * (1.0 + scale)) + shift


def _split_bf16(a):
    hi = a.astype(BF16)
    lo = (a - hi.astype(F32)).astype(BF16)
    return hi, lo


def _mod_kernel(c_ref, w_ref, b_ref, o_ref):
    s_hi, s_lo = _split_bf16(_silu(c_ref[...]))
    w_hi, w_lo = _split_bf16(w_ref[0])
    o_ref[0] = _dot(s_hi, w_hi) + _dot(s_lo, w_hi) + _dot(s_hi, w_lo) + b_ref[0]


def _modulation(cond, w_mod, b_mod):
    tn = 1536
    n_out = N_MOD * D_MODEL
    return pl.pallas_call(
        _mod_kernel,
        grid=(DEPTH, n_out // tn),
        in_specs=[
            pl.BlockSpec((COND_ROWS, D_MODEL), lambda l, n: (0, 0)),
            pl.BlockSpec((1, D_MODEL, tn), lambda l, n: (l, 0, n)),
            pl.BlockSpec((1, 1, tn), lambda l, n: (l, 0, n)),
        ],
        out_specs=pl.BlockSpec((1, COND_ROWS, tn), lambda l, n: (l, 0, n)),
        out_shape=jax.ShapeDtypeStruct((DEPTH, COND_ROWS, n_out), F32),
        compiler_params=pltpu.CompilerParams(
            dimension_semantics=("arbitrary", "arbitrary"), vmem_limit_bytes=VMEM_LIMIT),
        name="modulation",
    )(cond, w_mod, b_mod.reshape(DEPTH, 1, n_out))


def _expand_kv(ckv, kpe_rot, kpe_ss, wk_ref, wvt_ref, khn_ref, k_ref, vt_ref, rows):
    ckv_b = ckv.astype(BF16)
    kn_all = _dot(ckv_b, wk_ref[...])
    vt_all = lax.dot_general(wvt_ref[...], ckv_b, (((1,), (1,)), ((), ())),
                             preferred_element_type=F32)
    gain_nope = khn_ref[:, :QK_NOPE]
    for h in range(N_HEADS):
        kn = kn_all[:, h * QK_NOPE:(h + 1) * QK_NOPE]
        ss = jnp.sum(kn * kn, axis=-1, keepdims=True) + kpe_ss
        inv = lax.rsqrt(ss * (1.0 / QK_DIM) + EPS)
        k_ref[0, h, rows, :QK_NOPE] = (kn * inv * gain_nope).astype(BF16)
        k_ref[0, h, rows, QK_NOPE:] = (kpe_rot * inv).astype(BF16)
        vt_ref[0, h, :V_DIM, rows] = vt_all[h * V_DIM:(h + 1) * V_DIM, :].astype(BF16)
        vt_ref[0, h, V_DIM:, rows] = jnp.ones((VT_ROWS - V_DIM, kn.shape[0]), BF16)


def _rope(pe, cos, sin_a, sin_b):
    return (pe * cos + pltpu.roll(pe, PE_PAD - QK_ROPE // 4, axis=1) * sin_a
            + pltpu.roll(pe, QK_ROPE // 4, axis=1) * sin_b)


_MLA_WEIGHTS = ("wdq", "qn", "wuq", "wdkv", "kvn", "wk", "wvt", "qhn", "khn")


def _mla_proj_kernel(use_rope, emit_state, sub, *refs):
    refs = list(refs)
    n_in = 3 + len(_MLA_WEIGHTS)
    (x_ref, mod_ref, gmix_ref, wdq_ref, qn_ref, wuq_ref, wdkv_ref, kvn_ref, wk_ref, wvt_ref,
     qhn_ref, khn_ref) = refs[:n_in]
    refs = refs[n_in:]
    if use_rope:
        cos_ref, sina_ref, sinb_ref = refs[:3]
        refs = refs[3:]
    q_ref, k_ref, vt_ref = refs[:3]
    refs = refs[3:]

    m = mod_ref[0]
    gain = gmix_ref[...] * (1.0 + m[1:2])
    q_gain = qhn_ref[...] * (QK_DIM ** -0.5 * LOG2_E)
    tm = x_ref.shape[1]
    for r0 in range(0, tm, sub):
        rows = slice(r0, r0 + sub)
        h = (_rms(x_ref[0, rows, :], D_MODEL) * gain + m[0:1]).astype(BF16)

        kv = _dot(h, wdkv_ref[...])
        ckv = _rms(kv[:, :KV_LORA], KV_LORA) * kvn_ref[...]
        kpe = kv[:, KV_LORA:]
        if emit_state:
            ckv_ref, kpe_ref = refs
            ckv_ref[0, rows, :] = ckv
            kpe_ref[0, rows, :] = kpe[:, :QK_ROPE]
        kpe_ss = jnp.sum(kpe * kpe, axis=-1, keepdims=True)
        kpe_rot = kpe * khn_ref[:, QK_NOPE:]
        if use_rope:
            cos, sin_a, sin_b = cos_ref[rows, :], sina_ref[rows, :], sinb_ref[rows, :]
            kpe_rot = _rope(kpe_rot, cos, sin_a, sin_b)
        _expand_kv(ckv, kpe_rot, kpe_ss, wk_ref, wvt_ref, khn_ref, k_ref, vt_ref, rows)

        cq = (_rms(_dot(h, wdq_ref[...]), Q_LORA) * qn_ref[...]).astype(BF16)
        qf = _dot(cq, wuq_ref[...])
        for hd in range(N_HEADS):
            qh = qf[:, hd * HEAD_PAD:(hd + 1) * HEAD_PAD]
            inv = lax.rsqrt(jnp.sum(qh * qh, axis=-1, keepdims=True) * (1.0 / QK_DIM) + EPS)
            qn = qh * inv * q_gain
            q_ref[0, hd, rows, :QK_NOPE] = qn[:, :QK_NOPE].astype(BF16)
            qpe = qn[:, QK_NOPE:]
            if use_rope:
                qpe = _rope(qpe, cos, sin_a, sin_b)
            q_ref[0, hd, rows, QK_NOPE:] = qpe.astype(BF16)


def _cache_expand_kernel(ckv_ref, kpe_ref, wk_ref, wvt_ref, khn_ref, k_ref, vt_ref):
    kpe = kpe_ref[0]
    kpe_ss = jnp.sum(kpe * kpe, axis=-1, keepdims=True)
    _expand_kv(ckv_ref[0], kpe * khn_ref[:, QK_NOPE:], kpe_ss, wk_ref, wvt_ref, khn_ref,
               k_ref, vt_ref, slice(0, kpe.shape[0]))


def _const_spec(shape):
    nd = len(shape)
    return pl.BlockSpec(shape, lambda *_: (0,) * nd)


def _mla_proj(x, mod, cond_of_batch, gmix, w, tm, rope):
    b, t, _ = x.shape
    use_rope = rope is not None
    kern = functools.partial(_mla_proj_kernel, use_rope, not use_rope, min(tm, 256))
    in_specs = [
        pl.BlockSpec((1, tm, D_MODEL), lambda i, j: (i, j, 0)),
        pl.BlockSpec((1, N_MOD, D_MODEL), lambda i, j: (cond_of_batch(i), 0, 0)),
        _const_spec((1, D_MODEL)),
    ] + [_const_spec(w[n].shape) for n in _MLA_WEIGHTS]
    args = [x, mod, gmix] + [w[n] for n in _MLA_WEIGHTS]
    out_specs = [
        pl.BlockSpec((1, N_HEADS, tm, HEAD_PAD), lambda i, j: (i, 0, j, 0)),
        pl.BlockSpec((1, N_HEADS, tm, HEAD_PAD), lambda i, j: (i, 0, j, 0)),
        pl.BlockSpec((1, N_HEADS, VT_ROWS, tm), lambda i, j: (i, 0, 0, j)),
    ]
    out_shape = [
        jax.ShapeDtypeStruct((b, N_HEADS, t, HEAD_PAD), BF16),
        jax.ShapeDtypeStruct((b, N_HEADS, t, HEAD_PAD), BF16),
        jax.ShapeDtypeStruct((b, N_HEADS, VT_ROWS, t), BF16),
    ]
    if use_rope:
        in_specs += [pl.BlockSpec((tm, PE_PAD), lambda i, j: (j, 0))] * 3
        args += list(rope)
    else:
        out_specs += [pl.BlockSpec((1, tm, KV_LORA), lambda i, j: (i, j, 0)),
                      pl.BlockSpec((1, tm, QK_ROPE), lambda i, j: (i, j, 0))]
        out_shape += [jax.ShapeDtypeStruct((b, t, KV_LORA), F32),
                      jax.ShapeDtypeStruct((b, t, QK_ROPE), F32)]
    return pl.pallas_call(
        kern,
        grid=(b, t // tm),
        in_specs=in_specs,
        out_specs=out_specs,
        out_shape=out_shape,
        compiler_params=pltpu.CompilerParams(
            dimension_semantics=("arbitrary", "arbitrary"), vmem_limit_bytes=VMEM_LIMIT),
        name="mla_proj",
    )(*args)


def _cache_expand(cache_ckv, cache_kpe_pad, j, w):
    b, _, past, _ = cache_ckv.shape
    names = ("wk", "wvt", "khn")
    return pl.pallas_call(
        _cache_expand_kernel,
        grid=(b,),
        in_specs=[
            pl.BlockSpec((1, None, past, KV_LORA), lambda i: (i, j, 0, 0)),
            pl.BlockSpec((1, None, past, PE_PAD), lambda i: (i, j, 0, 0)),
        ] + [_const_spec(w[n].shape) for n in names],
        out_specs=[
            pl.BlockSpec((1, N_HEADS, past, HEAD_PAD), lambda i: (i, 0, 0, 0)),
            pl.BlockSpec((1, N_HEADS, VT_ROWS, past), lambda i: (i, 0, 0, 0)),
        ],
        out_shape=[
            jax.ShapeDtypeStruct((b, N_HEADS, past, HEAD_PAD), BF16),
            jax.ShapeDtypeStruct((b, N_HEADS, VT_ROWS, past), BF16),
        ],
        compiler_params=pltpu.CompilerParams(
            dimension_semantics=("arbitrary",), vmem_limit_bytes=VMEM_LIMIT),
        name="cache_expand",
    )(cache_ckv, cache_kpe_pad, *[w[n] for n in names])


def _attn_kernel(heads, chunk, has_cache, *refs):
    if has_cache:
        q_ref, k_ref, vt_ref, kc_ref, vtc_ref, o_ref, s_scr, p_scr = refs
    else:
        q_ref, k_ref, vt_ref, o_ref, s_scr, p_scr = refs
    tk = k_ref.shape[2]
    pieces = [[(k_ref, vt_ref, s0, min(chunk, tk - s0))] for s0 in range(0, tk, chunk)]
    if has_cache:
        pieces[-1].append((kc_ref, vtc_ref, 0, kc_ref.shape[2]))
    steps = [(h, piece) for h in range(heads) for piece in pieces]

    n_slots = s_scr.shape[0]

    def piece_rows(n):
        return sum(size for _, _, _, size in steps[n][1])

    def scores(n):
        h, piece = steps[n]
        row = 0
        for kr, _, s0, size in piece:
            s_scr[n % n_slots, row:row + size, :] = lax.dot_general(
                kr[0, h, s0:s0 + size, :], q_ref[0, h], (((1,), (1,)), ((), ())),
                preferred_element_type=F32)
            row += size

    def key_max(n):
        return jnp.max(s_scr[n % n_slots, :piece_rows(n), :], axis=0, keepdims=True)

    scores(0)
    if len(steps) > 1:
        scores(1)
    m_next = key_max(0)
    for n, (h, piece) in enumerate(steps):
        if n + 2 < len(steps):
            scores(n + 2)
        m_c = m_next
        if n + 1 < len(steps):
            m_next = key_max(n + 1)
        rows = piece_rows(n)
        s = s_scr[n % n_slots, :rows, :]
        first = n % len(pieces) == 0
        m_new = m_c if first else jnp.maximum(m, m_c)
        p_scr[n % 2, :rows, :] = jnp.exp2(s - m_new).astype(BF16)
        pv = None
        row = 0
        for _, vr, s0, size in piece:
            part = _dot(vr[0, h, :, s0:s0 + size], p_scr[n % 2, row:row + size, :])
            pv = part if pv is None else pv + part
            row += size
        acc = pv if first else jnp.exp2(m - m_new) * acc + pv
        m = m_new
        if (n + 1) % len(pieces) == 0:
            o = acc[:V_DIM] / acc[V_DIM:V_DIM + 1]
            o_ref[0, :, h * V_DIM:(h + 1) * V_DIM] = o.T.astype(BF16)


def _attention(q, k, vt, cache, tq, heads, chunk):
    b, _, t, _ = q.shape
    tk = k.shape[2]
    in_specs = [
        pl.BlockSpec((1, heads, tq, HEAD_PAD), lambda i, h, j: (i, h, j, 0)),
        pl.BlockSpec((1, heads, tk, HEAD_PAD), lambda i, h, j: (i, h, 0, 0)),
        pl.BlockSpec((1, heads, VT_ROWS, tk), lambda i, h, j: (i, h, 0, 0)),
    ]
    args = [q, k, vt]
    piece_rows = min(chunk, tk)
    if cache is not None:
        past = cache[0].shape[2]
        piece_rows = max(piece_rows, (tk - 1) % chunk + 1 + past)
        in_specs += [
            pl.BlockSpec((1, heads, past, HEAD_PAD), lambda i, h, j: (i, h, 0, 0)),
            pl.BlockSpec((1, heads, VT_ROWS, past), lambda i, h, j: (i, h, 0, 0)),
        ]
        args += list(cache)
    return pl.pallas_call(
        functools.partial(_attn_kernel, heads, chunk, cache is not None),
        grid=(b, N_HEADS // heads, t // tq),
        in_specs=in_specs,
        out_specs=pl.BlockSpec((1, tq, heads * V_DIM), lambda i, h, j: (i, j, h)),
        out_shape=jax.ShapeDtypeStruct((b, t, N_HEADS * V_DIM), BF16),
        scratch_shapes=[
            pltpu.VMEM((3, piece_rows, tq), F32),
            pltpu.VMEM((2, piece_rows, tq), BF16),
        ],
        compiler_params=pltpu.CompilerParams(
            dimension_semantics=("arbitrary", "arbitrary", "arbitrary"),
            vmem_limit_bytes=VMEM_LIMIT),
        name="attention",
    )(*args)


def _conv_kernel(tm, xp_ref, x_ref, xn_ref, mod_ref, gmix_ref, w1_ref, b1_ref, wdw_ref, bdw_ref,
                 lng_ref, lnb_ref, w2_ref, b2_ref, o_ref, gflat_ref, gt_ref, yflat_ref):
    j = pl.program_id(1)
    has_prev = j > 0
    has_next = j < pl.num_programs(1) - 1
    rows = tm + 2 * CONV_HALO
    m = mod_ref[0]
    gain = gmix_ref[...] * (1.0 + m[1:2])
    shift, gate = m[0:1], m[2:3]
    n_tiles = D_MODEL // 128
    gs = rows + 8
    ys = tm + 8
    first = CONV_HALO - CONV_PAD
    bias = bdw_ref[...]
    n_chunks = CONV_CHUNKS
    chunk = tm // n_chunks

    def stage_a(c):
        x_c = x_ref[0, c * chunk:(c + 1) * chunk, :]
        e0 = c * chunk + CONV_HALO
        parts = [x_c]
        if c == 0:
            parts, e0 = [xp_ref[0]] + parts, 0
        if c == n_chunks - 1:
            parts = parts + [xn_ref[0]]
        x_e = jnp.concatenate(parts, axis=0) if len(parts) > 1 else x_c
        size = x_e.shape[0]
        h = (_rms(x_e, D_MODEL) * gain + shift).astype(BF16)
        a = _dot(h, w1_ref[...]) + b1_ref[...]
        g = a[:, :D_MODEL] / (1.0 + jnp.exp(-a[:, D_MODEL:]))
        if c == 0 or c == n_chunks - 1:
            r = lax.broadcasted_iota(jnp.int32, (size, 1), 0) + e0
            valid = jnp.logical_and(jnp.logical_or(r >= CONV_HALO, has_prev),
                                    jnp.logical_or(r < tm + CONV_HALO, has_next))
            g = jnp.where(valid, g, 0.0)
        for t in range(n_tiles):
            gflat_ref[t * gs + e0:t * gs + e0 + size, :] = g[:, t * 128:(t + 1) * 128]
        for e in range(e0, e0 + size):
            gt_ref[e] = gflat_ref[pl.ds(e, n_tiles, stride=gs), :]

    def stage_c(c):
        tb = 16
        for t0 in range(c * chunk, (c + 1) * chunk, tb):
            accs = [bias] * tb
            for k in range(CONV_WIDTH):
                wk = wdw_ref[k]
                for u in range(tb):
                    accs[u] = accs[u] + gt_ref[t0 + u + first + k] * wk
            for u in range(tb):
                yflat_ref[pl.ds(t0 + u, n_tiles, stride=ys), :] = accs[u]

    def stage_d(c):
        r0 = c * chunk
        y = jnp.concatenate([yflat_ref[t * ys + r0:t * ys + r0 + chunk, :]
                             for t in range(n_tiles)], axis=-1)
        mu = jnp.mean(y, axis=-1, keepdims=True)
        yc = y - mu
        var = jnp.mean(yc * yc, axis=-1, keepdims=True)
        z = _silu(yc * lax.rsqrt(var + EPS) * lng_ref[...] + lnb_ref[...]).astype(BF16)
        o_ref[0, r0:r0 + chunk, :] = (x_ref[0, r0:r0 + chunk, :]
                                           + gate * (_dot(z, w2_ref[...]) + b2_ref[...]))

    stage_a(0)
    for c in range(n_chunks):
        if c + 1 < n_chunks:
            stage_a(c + 1)
        stage_c(c)
        stage_d(c)


def _conv_module(x, mod, cond_of_batch, gmix, w, tm):
    b, t, _ = x.shape
    nh = tm // CONV_HALO
    last = t // CONV_HALO - 1
    names = ("w1", "b1", "wdw", "bdw", "lng", "lnb", "w2", "b2")
    return pl.pallas_call(
        functools.partial(_conv_kernel, tm),
        grid=(b, t // tm),
        in_specs=[
            pl.BlockSpec((1, CONV_HALO, D_MODEL), lambda i, j: (i, jnp.maximum(j * nh - 1, 0), 0)),
            pl.BlockSpec((1, tm, D_MODEL), lambda i, j: (i, j, 0)),
            pl.BlockSpec((1, CONV_HALO, D_MODEL), lambda i, j: (i, jnp.minimum((j + 1) * nh, last), 0)),
            pl.BlockSpec((1, N_MOD, D_MODEL), lambda i, j: (cond_of_batch(i), 0, 0)),
            _const_spec((1, D_MODEL)),
        ] + [_const_spec(w[n].shape) for n in names],
        out_specs=pl.BlockSpec((1, tm, D_MODEL), lambda i, j: (i, j, 0)),
        out_shape=jax.ShapeDtypeStruct(x.shape, F32),
        scratch_shapes=[
            pltpu.VMEM((D_MODEL // 128 * (tm + 2 * CONV_HALO + 8), 128), F32),
            pltpu.VMEM((tm + 2 * CONV_HALO, D_MODEL // 128, 128), F32),
            pltpu.VMEM((D_MODEL // 128 * (tm + 8), 128), F32),
        ],
        compiler_params=pltpu.CompilerParams(
            dimension_semantics=("arbitrary", "arbitrary"), vmem_limit_bytes=VMEM_LIMIT),
        name="conv_module",
    )(x, x, x, mod, gmix, *[w[n] for n in names])


def _ffn_kernel(has_o, *refs):
    if has_o:
        x_ref, o_in_ref, wo_ref, mod_ref, gffn_ref, wg_ref, wu_ref, wd_ref, out_ref = refs
    else:
        x_ref, mod_ref, gffn_ref, wg_ref, wu_ref, wd_ref, out_ref = refs
    m = mod_ref[0]
    x = x_ref[...]
    if has_o:
        x = x + m[2:3] * _dot(o_in_ref[...], wo_ref[...])
    h = _mod_norm(x, gffn_ref[...], m[4:5], m[3:4]).astype(BF16)
    acc = jnp.zeros(x.shape, F32)
    for cs in (slice(0, FF_SPLIT), slice(FF_SPLIT, D_FF)):
        a = _dot(h, wg_ref[:, cs])
        u = _dot(h, wu_ref[:, cs])
        acc = acc + _dot((_silu(a) * u).astype(BF16), wd_ref[cs, :])
    out_ref[...] = x + m[5:6] * acc


def _ffn(x, o, wo, mod, cond_of_block, gffn, w, tm):
    t = x.shape[0]
    tok_spec = pl.BlockSpec((tm, D_MODEL), lambda i: (i, 0))
    single = pl.Buffered(1)
    in_specs = [tok_spec]
    args = [x]
    if o is not None:
        in_specs += [tok_spec, pl.BlockSpec(wo.shape, lambda i: (0, 0), pipeline_mode=single)]
        args += [o, wo]
    in_specs += [
        pl.BlockSpec((1, N_MOD, D_MODEL), lambda i: (cond_of_block(i), 0, 0)),
        _const_spec((1, D_MODEL)),
        pl.BlockSpec((D_MODEL, D_FF), lambda i: (0, 0), pipeline_mode=single),
        pl.BlockSpec((D_MODEL, D_FF), lambda i: (0, 0), pipeline_mode=single),
        pl.BlockSpec((D_FF, D_MODEL), lambda i: (0, 0), pipeline_mode=single),
    ]
    args += [mod, gffn, w["wg"], w["wu"], w["wd"]]
    return pl.pallas_call(
        functools.partial(_ffn_kernel, o is not None),
        grid=(t // tm,),
        in_specs=in_specs,
        out_specs=tok_spec,
        out_shape=jax.ShapeDtypeStruct(x.shape, F32),
        compiler_params=pltpu.CompilerParams(
            dimension_semantics=("arbitrary",), vmem_limit_bytes=VMEM_LIMIT),
        name="ffn",
    )(*args)


def _rope_tables(t):
    f32 = np.float32
    rows = t // GRID_W
    row = np.repeat(np.arange(rows), GRID_W).astype(f32)
    col = (np.arange(t) % GRID_W).astype(f32)
    ax = QK_ROPE // 2
    inv = (f32(1.0) / (f32(ROPE_THETA) ** (np.arange(0, ax, 2, dtype=f32) / f32(ax)))).astype(f32)
    ar = row[:, None] * inv[None, :]
    ac = col[:, None] * inv[None, :]
    cos = np.concatenate([np.cos(ar), np.cos(ar), np.cos(ac), np.cos(ac)], axis=-1)
    sin = np.concatenate([np.sin(ar), np.sin(ar), np.sin(ac), np.sin(ac)], axis=-1)
    q = QK_ROPE // 4
    first_half = (np.arange(QK_ROPE) // q) % 2 == 0
    sin_a = np.where(first_half, -sin, f32(0.0))
    sin_b = np.where(first_half, f32(0.0), sin)
    pad = ((0, 0), (0, PE_PAD - QK_ROPE))
    return tuple(jnp.asarray(np.pad(a, pad).astype(f32)) for a in (cos, sin_a, sin_b))


def _row(v):
    return v.reshape(1, -1)


def kernel(x_prompt, x_sample, cache_ckv, cache_kpe, c, c_ctx, w_mod, b_mod, norm_mix, norm_ffn, mla_w_dq, mla_q_norm, mla_w_uq, mla_w_dkv, mla_kv_norm, mla_w_ukv, mla_q_hnorm, mla_k_hnorm, mla_w_o, conv_w_pw1, conv_b_pw1, conv_w_dw, conv_b_dw, conv_ln_g, conv_ln_b, conv_w_pw2, conv_b_pw2, ffn_w_gate, ffn_w_up, ffn_w_down):
    bp, tp, _ = x_prompt.shape
    bs, ts, _ = x_sample.shape

    cond = jnp.zeros((COND_ROWS, D_MODEL), F32).at[0].set(c_ctx).at[1:1 + bs].set(c)
    mod_all = _modulation(cond, w_mod, b_mod).reshape(DEPTH, COND_ROWS, N_MOD, D_MODEL)

    rope = _rope_tables(ts)
    cache_kpe_pad = jnp.pad(cache_kpe, ((0, 0), (0, 0), (0, 0), (0, PE_PAD - QK_ROPE)))

    ffn_tm = 512
    blocks_per_sample = ts // ffn_tm
    cond_prompt = lambda i: 0
    cond_sample_blk = lambda i: 1 + i // blocks_per_sample
    cond_sample_batch = lambda i: 1 + i

    xp = x_prompt
    xs = x_sample
    new_ckv, new_kpe = [], []
    for l in range(DEPTH):
        mod = mod_all[l]
        gmix = _row(norm_mix[l])
        gffn = _row(norm_ffn[l])
        fw = {"wg": ffn_w_gate[l].astype(BF16), "wu": ffn_w_up[l].astype(BF16),
              "wd": ffn_w_down[l].astype(BF16)}
        j = l // 2
        if l % 2 == 0:
            wuq = mla_w_uq[j].reshape(Q_LORA, N_HEADS, QK_DIM)
            wuq = jnp.pad(wuq, ((0, 0), (0, 0), (0, HEAD_PAD - QK_DIM)))
            wukv = mla_w_ukv[j].reshape(KV_LORA, N_HEADS, QK_NOPE + V_DIM)
            mw = {
                "wdq": mla_w_dq[j].astype(BF16),
                "qn": _row(mla_q_norm[j]),
                "wuq": wuq.reshape(Q_LORA, N_HEADS * HEAD_PAD).astype(BF16),
                "wdkv": jnp.pad(mla_w_dkv[j], ((0, 0), (0, PE_PAD - QK_ROPE))).astype(BF16),
                "kvn": _row(mla_kv_norm[j]),
                "wk": wukv[:, :, :QK_NOPE].reshape(KV_LORA, N_HEADS * QK_NOPE).astype(BF16),
                "wvt": wukv[:, :, QK_NOPE:].reshape(KV_LORA, N_HEADS * V_DIM).T.astype(BF16),
                "qhn": _row(jnp.pad(mla_q_hnorm[j], (0, HEAD_PAD - QK_DIM))),
                "khn": _row(jnp.pad(mla_k_hnorm[j], (0, HEAD_PAD - QK_DIM))),
            }
            wo = mla_w_o[j].astype(BF16)
            qp, kp, vtp, ckv_p, kpe_p = _mla_proj(xp, mod, cond_prompt, gmix, mw, tp, None)
            new_ckv.append(ckv_p)
            new_kpe.append(kpe_p)
            op = _attention(qp, kp, vtp, None, tp, N_HEADS, tp)
            cache = _cache_expand(cache_ckv, cache_kpe_pad, j, mw)
            qs, ks, vts = _mla_proj(xs, mod, cond_sample_batch, gmix, mw, 512, rope)
            os_ = _attention(qs, ks, vts, cache, 512, 4, 1024)
            xp2 = _ffn(xp.reshape(bp * tp, D_MODEL), op.reshape(bp * tp, D_MODEL), wo, mod,
                       cond_prompt, gffn, fw, ffn_tm)
            xs2 = _ffn(xs.reshape(bs * ts, D_MODEL), os_.reshape(bs * ts, D_MODEL), wo, mod,
                       cond_sample_blk, gffn, fw, ffn_tm)
        else:
            cw = {
                "w1": conv_w_pw1[j].astype(BF16), "b1": _row(conv_b_pw1[j]),
                "wdw": conv_w_dw[j].reshape(CONV_WIDTH, D_MODEL // 128, 128),
                "bdw": conv_b_dw[j].reshape(D_MODEL // 128, 128),
                "lng": _row(conv_ln_g[j]), "lnb": _row(conv_ln_b[j]),
                "w2": conv_w_pw2[j].astype(BF16), "b2": _row(conv_b_pw2[j]),
            }
            xp1 = _conv_module(xp, mod, cond_prompt, gmix, cw, tp)
            xs1 = _conv_module(xs, mod, cond_sample_batch, gmix, cw, 512)
            xp2 = _ffn(xp1.reshape(bp * tp, D_MODEL), None, None, mod, cond_prompt, gffn, fw, ffn_tm)
            xs2 = _ffn(xs1.reshape(bs * ts, D_MODEL), None, None, mod, cond_sample_blk, gffn, fw,
                       ffn_tm)
        xp = xp2.reshape(bp, tp, D_MODEL)
        xs = xs2.reshape(bs, ts, D_MODEL)

    state_ckv = jnp.stack(new_ckv, axis=1)
    state_kpe = jnp.stack(new_kpe, axis=1)
    return (xp, xs, state_ckv, state_kpe)
```

```python
import functools

import jax
import jax.numpy as jnp
import numpy as np
from jax import lax
from jax.experimental import pallas as pl
from jax.experimental.pallas import tpu as pltpu

F32 = jnp.float32
BF16 = jnp.bfloat16

D_MODEL = 1024
DEPTH = 4
N_MOD = 6
GRID_W = 64
N_HEADS = 8
Q_LORA = 512
KV_LORA = 256
QK_NOPE = 128
QK_ROPE = 64
QK_DIM = QK_NOPE + QK_ROPE
V_DIM = 128
VT_ROWS = V_DIM + 16
HEAD_PAD = 256
PE_PAD = HEAD_PAD - QK_NOPE
ROPE_THETA = 10000.0
CONV_WIDTH = 31
CONV_PAD = (CONV_WIDTH - 1) // 2
CONV_HALO = 16
CONV_CHUNKS = 2
D_FF = 2816
MXU_TILE = 256
FF_SPLIT = 6 * MXU_TILE
EPS = 1e-6
LOG2_E = 1.4426950408889634
COND_ROWS = 8

VMEM_LIMIT = 56 * 1024 * 1024


def _dot(a, b):
    return jnp.dot(a, b, preferred_element_type=F32)


def _silu(a):
    return a / (1.0 + jnp.exp(-a))


def _rms(x, n):
    return x * lax.rsqrt(jnp.sum(x * x, axis=-1, keepdims=True) * (1.0 / n) + EPS)


def _mod_norm(x, gain, scale, shift):
    return _rms(x, x.shape[-1]) * (gain * (1.0 + scale)) + shift


def _split_bf16(a):
    hi = a.astype(BF16)
    lo = (a - hi.astype(F32)).astype(BF16)
    return hi, lo


def _mod_kernel(c_ref, w_ref, b_ref, o_ref):
    s_hi, s_lo = _split_bf16(_silu(c_ref[...]))
    w_hi, w_lo = _split_bf16(w_ref[0])
    o_ref[0] = _dot(s_hi, w_hi) + _dot(s_lo, w_hi) + _dot(s_hi, w_lo) + b_ref[0]


def _modulation(cond, w_mod, b_mod):
    tn = 1536
    n_out = N_MOD * D_MODEL
    return pl.pallas_call(
        _mod_kernel,
        grid=(DEPTH, n_out // tn),
        in_specs=[
            pl.BlockSpec((COND_ROWS, D_MODEL), lambda l, n: (0, 0)),
            pl.BlockSpec((1, D_MODEL, tn), lambda l, n: (l, 0, n)),
            pl.BlockSpec((1, 1, tn), lambda l, n: (l, 0, n)),
        ],
        out_specs=pl.BlockSpec((1, COND_ROWS, tn), lambda l, n: (l, 0, n)),
        out_shape=jax.ShapeDtypeStruct((DEPTH, COND_ROWS, n_out), F32),
        compiler_params=pltpu.CompilerParams(
            dimension_semantics=("arbitrary", "arbitrary"), vmem_limit_bytes=VMEM_LIMIT),
        name="modulation",
    )(cond, w_mod, b_mod.reshape(DEPTH, 1, n_out))


def _expand_kv(ckv, kpe_rot, kpe_ss, wk_ref, wvt_ref, khn_ref, k_ref, vt_ref, rows):
    ckv_b = ckv.astype(BF16)
    kn_all = _dot(ckv_b, wk_ref[...])
    vt_all = lax.dot_general(wvt_ref[...], ckv_b, (((1,), (1,)), ((), ())),
                             preferred_element_type=F32)
    gain_nope = khn_ref[:, :QK_NOPE]
    for h in range(N_HEADS):
        kn = kn_all[:, h * QK_NOPE:(h + 1) * QK_NOPE]
        ss = jnp.sum(kn * kn, axis=-1, keepdims=True) + kpe_ss
        inv = lax.rsqrt(ss * (1.0 / QK_DIM) + EPS)
        k_ref[0, h, rows, :QK_NOPE] = (kn * inv * gain_nope).astype(BF16)
        k_ref[0, h, rows, QK_NOPE:] = (kpe_rot * inv).astype(BF16)
        vt_ref[0, h, :V_DIM, rows] = vt_all[h * V_DIM:(h + 1) * V_DIM, :].astype(BF16)
        vt_ref[0, h, V_DIM:, rows] = jnp.ones((VT_ROWS - V_DIM, kn.shape[0]), BF16)


def _rope(pe, cos, sin_a, sin_b):
    return (pe * cos + pltpu.roll(pe, PE_PAD - QK_ROPE // 4, axis=1) * sin_a
            + pltpu.roll(pe, QK_ROPE // 4, axis=1) * sin_b)


_MLA_WEIGHTS = ("wdq", "qn", "wuq", "wdkv", "kvn", "wk", "wvt", "qhn", "khn")


def _mla_proj_kernel(use_rope, emit_state, sub, *refs):
    refs = list(refs)
    n_in = 3 + len(_MLA_WEIGHTS)
    (x_ref, mod_ref, gmix_ref, wdq_ref, qn_ref, wuq_ref, wdkv_ref, kvn_ref, wk_ref, wvt_ref,
     qhn_ref, khn_ref) = refs[:n_in]
    refs = refs[n_in:]
    if use_rope:
        cos_ref, sina_ref, sinb_ref = refs[:3]
        refs = refs[3:]
    q_ref, k_ref, vt_ref = refs[:3]
    refs = refs[3:]

    m = mod_ref[0]
    gain = gmix_ref[...] * (1.0 + m[1:2])
    q_gain = qhn_ref[...] * (QK_DIM ** -0.5 * LOG2_E)
    tm = x_ref.shape[1]
    for r0 in range(0, tm, sub):
        rows = slice(r0, r0 + sub)
        h = (_rms(x_ref[0, rows, :], D_MODEL) * gain + m[0:1]).astype(BF16)

        kv = _dot(h, wdkv_ref[...])
        ckv = _rms(kv[:, :KV_LORA], KV_LORA) * kvn_ref[...]
        kpe = kv[:, KV_LORA:]
        if emit_state:
            ckv_ref, kpe_ref = refs
            ckv_ref[0, rows, :] = ckv
            kpe_ref[0, rows, :] = kpe[:, :QK_ROPE]
        kpe_ss = jnp.sum(kpe * kpe, axis=-1, keepdims=True)
        kpe_rot = kpe * khn_ref[:, QK_NOPE:]
        if use_rope:
            cos, sin_a, sin_b = cos_ref[rows, :], sina_ref[rows, :], sinb_ref[rows, :]
            kpe_rot = _rope(kpe_rot, cos, sin_a, sin_b)
        _expand_kv(ckv, kpe_rot, kpe_ss, wk_ref, wvt_ref, khn_ref, k_ref, vt_ref, rows)

        cq = (_rms(_dot(h, wdq_ref[...]), Q_LORA) * qn_ref[...]).astype(BF16)
        qf = _dot(cq, wuq_ref[...])
        for hd in range(N_HEADS):
            qh = qf[:, hd * HEAD_PAD:(hd + 1) * HEAD_PAD]
            inv = lax.rsqrt(jnp.sum(qh * qh, axis=-1, keepdims=True) * (1.0 / QK_DIM) + EPS)
            qn = qh * inv * q_gain
            q_ref[0, hd, rows, :QK_NOPE] = qn[:, :QK_NOPE].astype(BF16)
            qpe = qn[:, QK_NOPE:]
            if use_rope:
                qpe = _rope(qpe, cos, sin_a, sin_b)
            q_ref[0, hd, rows, QK_NOPE:] = qpe.astype(BF16)


def _cache_expand_kernel(ckv_ref, kpe_ref, wk_ref, wvt_ref, khn_ref, k_ref, vt_ref):
    kpe = kpe_ref[0]
    kpe_ss = jnp.sum(kpe * kpe, axis=-1, keepdims=True)
    _expand_kv(ckv_ref[0], kpe * khn_ref[:, QK_NOPE:], kpe_ss, wk_ref, wvt_ref, khn_ref,
               k_ref, vt_ref, slice(0, kpe.shape[0]))


def _const_spec(shape):
    nd = len(shape)
    return pl.BlockSpec(shape, lambda *_: (0,) * nd)


def _layer_spec(stack, layer, **kw):
    nd = stack.ndim - 1
    return pl.BlockSpec((None,) + stack.shape[1:], lambda *_: (layer,) + (0,) * nd, **kw)


def _mla_proj(x, mod, cond_of_batch, gmix, w, tm, rope):
    b, t, _ = x.shape
    use_rope = rope is not None
    kern = functools.partial(_mla_proj_kernel, use_rope, not use_rope, min(tm, 256))
    in_specs = [
        pl.BlockSpec((1, tm, D_MODEL), lambda i, j: (i, j, 0)),
        pl.BlockSpec((1, N_MOD, D_MODEL), lambda i, j: (cond_of_batch(i), 0, 0)),
        _const_spec((1, D_MODEL)),
    ] + [_const_spec(w[n].shape) for n in _MLA_WEIGHTS]
    args = [x, mod, gmix] + [w[n] for n in _MLA_WEIGHTS]
    out_specs = [
        pl.BlockSpec((1, N_HEADS, tm, HEAD_PAD), lambda i, j: (i, 0, j, 0)),
        pl.BlockSpec((1, N_HEADS, tm, HEAD_PAD), lambda i, j: (i, 0, j, 0)),
        pl.BlockSpec((1, N_HEADS, VT_ROWS, tm), lambda i, j: (i, 0, 0, j)),
    ]
    out_shape = [
        jax.ShapeDtypeStruct((b, N_HEADS, t, HEAD_PAD), BF16),
        jax.ShapeDtypeStruct((b, N_HEADS, t, HEAD_PAD), BF16),
        jax.ShapeDtypeStruct((b, N_HEADS, VT_ROWS, t), BF16),
    ]
    if use_rope:
        in_specs += [pl.BlockSpec((tm, PE_PAD), lambda i, j: (j, 0))] * 3
        args += list(rope)
    else:
        out_specs += [pl.BlockSpec((1, tm, KV_LORA), lambda i, j: (i, j, 0)),
                      pl.BlockSpec((1, tm, QK_ROPE), lambda i, j: (i, j, 0))]
        out_shape += [jax.ShapeDtypeStruct((b, t, KV_LORA), F32),
                      jax.ShapeDtypeStruct((b, t, QK_ROPE), F32)]
    return pl.pallas_call(
        kern,
        grid=(b, t // tm),
        in_specs=in_specs,
        out_specs=out_specs,
        out_shape=out_shape,
        compiler_params=pltpu.CompilerParams(
            dimension_semantics=("arbitrary", "arbitrary"), vmem_limit_bytes=VMEM_LIMIT),
        name="mla_proj",
    )(*args)


def _cache_expand(cache_ckv, cache_kpe_pad, j, w):
    b, _, past, _ = cache_ckv.shape
    names = ("wk", "wvt", "khn")
    return pl.pallas_call(
        _cache_expand_kernel,
        grid=(b,),
        in_specs=[
            pl.BlockSpec((1, None, past, KV_LORA), lambda i: (i, j, 0, 0)),
            pl.BlockSpec((1, None, past, PE_PAD), lambda i: (i, j, 0, 0)),
        ] + [_const_spec(w[n].shape) for n in names],
        out_specs=[
            pl.BlockSpec((1, N_HEADS, past, HEAD_PAD), lambda i: (i, 0, 0, 0)),
            pl.BlockSpec((1, N_HEADS, VT_ROWS, past), lambda i: (i, 0, 0, 0)),
        ],
        out_shape=[
            jax.ShapeDtypeStruct((b, N_HEADS, past, HEAD_PAD), BF16),
            jax.ShapeDtypeStruct((b, N_HEADS, VT_ROWS, past), BF16),
        ],
        compiler_params=pltpu.CompilerParams(
            dimension_semantics=("arbitrary",), vmem_limit_bytes=VMEM_LIMIT),
        name="cache_expand",
    )(cache_ckv, cache_kpe_pad, *[w[n] for n in names])


def _attn_kernel(heads, chunk, has_cache, *refs):
    if has_cache:
        q_ref, k_ref, vt_ref, kc_ref, vtc_ref, o_ref, s_scr, p_scr = refs
    else:
        q_ref, k_ref, vt_ref, o_ref, s_scr, p_scr = refs
    tk = k_ref.shape[2]
    pieces = [[(k_ref, vt_ref, s0, min(chunk, tk - s0))] for s0 in range(0, tk, chunk)]
    if has_cache:
        pieces[-1].append((kc_ref, vtc_ref, 0, kc_ref.shape[2]))
    steps = [(h, piece) for h in range(heads) for piece in pieces]

    n_slots = s_scr.shape[0]

    def piece_rows(n):
        return sum(size for _, _, _, size in steps[n][1])

    def scores(n):
        h, piece = steps[n]
        row = 0
        for kr, _, s0, size in piece:
            s_scr[n % n_slots, row:row + size, :] = lax.dot_general(
                kr[0, h, s0:s0 + size, :], q_ref[0, h], (((1,), (1,)), ((), ())),
                preferred_element_type=F32)
            row += size

    def key_max(n):
        return jnp.max(s_scr[n % n_slots, :piece_rows(n), :], axis=0, keepdims=True)

    scores(0)
    if len(steps) > 1:
        scores(1)
    m_next = key_max(0)
    for n, (h, piece) in enumerate(steps):
        if n + 2 < len(steps):
            scores(n + 2)
        m_c = m_next
        if n + 1 < len(steps):
            m_next = key_max(n + 1)
        rows = piece_rows(n)
        s = s_scr[n % n_slots, :rows, :]
        first = n % len(pieces) == 0
        m_new = m_c if first else jnp.maximum(m, m_c)
        p_scr[n % 2, :rows, :] = jnp.exp2(s - m_new).astype(BF16)
        pv = None
        row = 0
        for _, vr, s0, size in piece:
            part = _dot(vr[0, h, :, s0:s0 + size], p_scr[n % 2, row:row + size, :])
            pv = part if pv is None else pv + part
            row += size
        acc = pv if first else jnp.exp2(m - m_new) * acc + pv
        m = m_new
        if (n + 1) % len(pieces) == 0:
            o = acc[:V_DIM] / acc[V_DIM:V_DIM + 1]
            o_ref[0, :, h * V_DIM:(h + 1) * V_DIM] = o.T.astype(BF16)


def _attention(q, k, vt, cache, tq, heads, chunk):
    b, _, t, _ = q.shape
    tk = k.shape[2]
    in_specs = [
        pl.BlockSpec((1, heads, tq, HEAD_PAD), lambda i, h, j: (i, h, j, 0)),
        pl.BlockSpec((1, heads, tk, HEAD_PAD), lambda i, h, j: (i, h, 0, 0)),
        pl.BlockSpec((1, heads, VT_ROWS, tk), lambda i, h, j: (i, h, 0, 0)),
    ]
    args = [q, k, vt]
    piece_rows = min(chunk, tk)
    if cache is not None:
        past = cache[0].shape[2]
        piece_rows = max(piece_rows, (tk - 1) % chunk + 1 + past)
        in_specs += [
            pl.BlockSpec((1, heads, past, HEAD_PAD), lambda i, h, j: (i, h, 0, 0)),
            pl.BlockSpec((1, heads, VT_ROWS, past), lambda i, h, j: (i, h, 0, 0)),
        ]
        args += list(cache)
    return pl.pallas_call(
        functools.partial(_attn_kernel, heads, chunk, cache is not None),
        grid=(b, N_HEADS // heads, t // tq),
        in_specs=in_specs,
        out_specs=pl.BlockSpec((1, tq, heads * V_DIM), lambda i, h, j: (i, j, h)),
        out_shape=jax.ShapeDtypeStruct((b, t, N_HEADS * V_DIM), BF16),
        scratch_shapes=[
            pltpu.VMEM((3, piece_rows, tq), F32),
            pltpu.VMEM((2, piece_rows, tq), BF16),
        ],
        compiler_params=pltpu.CompilerParams(
            dimension_semantics=("arbitrary", "arbitrary", "arbitrary"),
            vmem_limit_bytes=VMEM_LIMIT),
        name="attention",
    )(*args)


def _conv_kernel(tm, xp_ref, x_ref, xn_ref, mod_ref, gmix_ref, w1_ref, b1_ref, wdw_ref, bdw_ref,
                 lng_ref, lnb_ref, w2_ref, b2_ref, o_ref, gflat_ref, gt_ref, yflat_ref):
    j = pl.program_id(1)
    has_prev = j > 0
    has_next = j < pl.num_programs(1) - 1
    rows = tm + 2 * CONV_HALO
    m = mod_ref[0]
    gain = gmix_ref[...] * (1.0 + m[1:2])
    shift, gate = m[0:1], m[2:3]
    n_tiles = D_MODEL // 128
    gs = rows + 8
    ys = tm + 8
    first = CONV_HALO - CONV_PAD
    bias = bdw_ref[...]
    n_chunks = CONV_CHUNKS
    chunk = tm // n_chunks

    def stage_a(c):
        x_c = x_ref[0, c * chunk:(c + 1) * chunk, :]
        e0 = c * chunk + CONV_HALO
        parts = [x_c]
        if c == 0:
            parts, e0 = [xp_ref[0]] + parts, 0
        if c == n_chunks - 1:
            parts = parts + [xn_ref[0]]
        x_e = jnp.concatenate(parts, axis=0) if len(parts) > 1 else x_c
        size = x_e.shape[0]
        h = (_rms(x_e, D_MODEL) * gain + shift).astype(BF16)
        a = _dot(h, w1_ref[...]) + b1_ref[...]
        g = a[:, :D_MODEL] / (1.0 + jnp.exp(-a[:, D_MODEL:]))
        if c == 0 or c == n_chunks - 1:
            r = lax.broadcasted_iota(jnp.int32, (size, 1), 0) + e0
            valid = jnp.logical_and(jnp.logical_or(r >= CONV_HALO, has_prev),
                                    jnp.logical_or(r < tm + CONV_HALO, has_next))
            g = jnp.where(valid, g, 0.0)
        for t in range(n_tiles):
            gflat_ref[t * gs + e0:t * gs + e0 + size, :] = g[:, t * 128:(t + 1) * 128]
        for e in range(e0, e0 + size):
            gt_ref[e] = gflat_ref[pl.ds(e, n_tiles, stride=gs), :]

    def stage_c(c):
        tb = 16
        for t0 in range(c * chunk, (c + 1) * chunk, tb):
            accs = [bias] * tb
            for k in range(CONV_WIDTH):
                wk = wdw_ref[k]
                for u in range(tb):
                    accs[u] = accs[u] + gt_ref[t0 + u + first + k] * wk
            for u in range(tb):
                yflat_ref[pl.ds(t0 + u, n_tiles, stride=ys), :] = accs[u]

    def stage_d(c):
        r0 = c * chunk
        y = jnp.concatenate([yflat_ref[t * ys + r0:t * ys + r0 + chunk, :]
                             for t in range(n_tiles)], axis=-1)
        mu = jnp.mean(y, axis=-1, keepdims=True)
        yc = y - mu
        var = jnp.mean(yc * yc, axis=-1, keepdims=True)
        z = _silu(yc * lax.rsqrt(var + EPS) * lng_ref[...] + lnb_ref[...]).astype(BF16)
        o_ref[0, r0:r0 + chunk, :] = (x_ref[0, r0:r0 + chunk, :]
                                           + gate * (_dot(z, w2_ref[...]) + b2_ref[...]))

    stage_a(0)
    for c in range(n_chunks):
        if c + 1 < n_chunks:
            stage_a(c + 1)
        stage_c(c)
        stage_d(c)


def _conv_module(x, mod, cond_of_batch, gmix, w, layer, tm):
    b, t, _ = x.shape
    nh = tm // CONV_HALO
    last = t // CONV_HALO - 1
    names = ("w1", "b1", "wdw", "bdw", "lng", "lnb", "w2", "b2")
    return pl.pallas_call(
        functools.partial(_conv_kernel, tm),
        grid=(b, t // tm),
        in_specs=[
            pl.BlockSpec((1, CONV_HALO, D_MODEL), lambda i, j: (i, jnp.maximum(j * nh - 1, 0), 0)),
            pl.BlockSpec((1, tm, D_MODEL), lambda i, j: (i, j, 0)),
            pl.BlockSpec((1, CONV_HALO, D_MODEL), lambda i, j: (i, jnp.minimum((j + 1) * nh, last), 0)),
            pl.BlockSpec((1, N_MOD, D_MODEL), lambda i, j: (cond_of_batch(i), 0, 0)),
            _const_spec((1, D_MODEL)),
        ] + [_layer_spec(w[n], layer) if n in ("w1", "w2") else _const_spec(w[n].shape)
             for n in names],
        out_specs=pl.BlockSpec((1, tm, D_MODEL), lambda i, j: (i, j, 0)),
        out_shape=jax.ShapeDtypeStruct(x.shape, F32),
        scratch_shapes=[
            pltpu.VMEM((D_MODEL // 128 * (tm + 2 * CONV_HALO + 8), 128), F32),
            pltpu.VMEM((tm + 2 * CONV_HALO, D_MODEL // 128, 128), F32),
            pltpu.VMEM((D_MODEL // 128 * (tm + 8), 128), F32),
        ],
        compiler_params=pltpu.CompilerParams(
            dimension_semantics=("arbitrary", "arbitrary"), vmem_limit_bytes=VMEM_LIMIT),
        name="conv_module",
    )(x, x, x, mod, gmix, *[w[n] for n in names])


def _ffn_kernel(has_o, *refs):
    if has_o:
        x_ref, o_in_ref, wo_ref, mod_ref, gffn_ref, wg_ref, wu_ref, wd_ref, out_ref = refs
    else:
        x_ref, mod_ref, gffn_ref, wg_ref, wu_ref, wd_ref, out_ref = refs
    m = mod_ref[0]
    x = x_ref[...]
    if has_o:
        x = x + m[2:3] * _dot(o_in_ref[...], wo_ref[...])
    h = _mod_norm(x, gffn_ref[...], m[4:5], m[3:4]).astype(BF16)
    acc = jnp.zeros(x.shape, F32)
    for cs in (slice(0, FF_SPLIT), slice(FF_SPLIT, D_FF)):
        a = _dot(h, wg_ref[:, cs])
        u = _dot(h, wu_ref[:, cs])
        acc = acc + _dot((_silu(a) * u).astype(BF16), wd_ref[cs, :])
    out_ref[...] = x + m[5:6] * acc


def _ffn(x, o, wo, mla_layer, mod, cond_of_block, gffn, w, layer, tm):
    t = x.shape[0]
    tok_spec = pl.BlockSpec((tm, D_MODEL), lambda i: (i, 0))
    single = pl.Buffered(1)
    in_specs = [tok_spec]
    args = [x]
    if o is not None:
        in_specs += [tok_spec, _layer_spec(wo, mla_layer, pipeline_mode=single)]
        args += [o, wo]
    in_specs += [
        pl.BlockSpec((1, N_MOD, D_MODEL), lambda i: (cond_of_block(i), 0, 0)),
        _const_spec((1, D_MODEL)),
    ] + [_layer_spec(w[n], layer, pipeline_mode=single) for n in ("wg", "wu", "wd")]
    args += [mod, gffn, w["wg"], w["wu"], w["wd"]]
    return pl.pallas_call(
        functools.partial(_ffn_kernel, o is not None),
        grid=(t // tm,),
        in_specs=in_specs,
        out_specs=tok_spec,
        out_shape=jax.ShapeDtypeStruct(x.shape, F32),
        compiler_params=pltpu.CompilerParams(
            dimension_semantics=("arbitrary",), vmem_limit_bytes=VMEM_LIMIT),
        name="ffn",
    )(*args)


def _rope_tables(t):
    f32 = np.float32
    rows = t // GRID_W
    row = np.repeat(np.arange(rows), GRID_W).astype(f32)
    col = (np.arange(t) % GRID_W).astype(f32)
    ax = QK_ROPE // 2
    inv = (f32(1.0) / (f32(ROPE_THETA) ** (np.arange(0, ax, 2, dtype=f32) / f32(ax)))).astype(f32)
    ar = row[:, None] * inv[None, :]
    ac = col[:, None] * inv[None, :]
    cos = np.concatenate([np.cos(ar), np.cos(ar), np.cos(ac), np.cos(ac)], axis=-1)
    sin = np.concatenate([np.sin(ar), np.sin(ar), np.sin(ac), np.sin(ac)], axis=-1)
    q = QK_ROPE // 4
    first_half = (np.arange(QK_ROPE) // q) % 2 == 0
    sin_a = np.where(first_half, -sin, f32(0.0))
    sin_b = np.where(first_half, f32(0.0), sin)
    pad = ((0, 0), (0, PE_PAD - QK_ROPE))
    return tuple(jnp.asarray(np.pad(a, pad).astype(f32)) for a in (cos, sin_a, sin_b))


def _row(v):
    return v.reshape(1, -1)


def kernel(x_prompt, x_sample, cache_ckv, cache_kpe, c, c_ctx, w_mod, b_mod, norm_mix, norm_ffn, mla_w_dq, mla_q_norm, mla_w_uq, mla_w_dkv, mla_kv_norm, mla_w_ukv, mla_q_hnorm, mla_k_hnorm, mla_w_o, conv_w_pw1, conv_b_pw1, conv_w_dw, conv_b_dw, conv_ln_g, conv_ln_b, conv_w_pw2, conv_b_pw2, ffn_w_gate, ffn_w_up, ffn_w_down):
    bp, tp, _ = x_prompt.shape
    bs, ts, _ = x_sample.shape

    cond = jnp.zeros((COND_ROWS, D_MODEL), F32).at[0].set(c_ctx).at[1:1 + bs].set(c)
    mod_all = _modulation(cond, w_mod, b_mod).reshape(DEPTH, COND_ROWS, N_MOD, D_MODEL)

    rope = _rope_tables(ts)
    cache_kpe_pad = jnp.pad(cache_kpe, ((0, 0), (0, 0), (0, 0), (0, PE_PAD - QK_ROPE)))

    ffn_tm = 512
    blocks_per_sample = ts // ffn_tm
    cond_prompt = lambda i: 0
    cond_sample_blk = lambda i: 1 + i // blocks_per_sample
    cond_sample_batch = lambda i: 1 + i

    fw = {"wg": ffn_w_gate.astype(BF16), "wu": ffn_w_up.astype(BF16), "wd": ffn_w_down.astype(BF16)}
    wo = mla_w_o.astype(BF16)
    conv_w1 = conv_w_pw1.astype(BF16)
    conv_w2 = conv_w_pw2.astype(BF16)

    xp = x_prompt
    xs = x_sample
    new_ckv, new_kpe = [], []
    for l in range(DEPTH):
        mod = mod_all[l]
        gmix = _row(norm_mix[l])
        gffn = _row(norm_ffn[l])
        j = l // 2
        if l % 2 == 0:
            wuq = mla_w_uq[j].reshape(Q_LORA, N_HEADS, QK_DIM)
            wuq = jnp.pad(wuq, ((0, 0), (0, 0), (0, HEAD_PAD - QK_DIM)))
            wukv = mla_w_ukv[j].reshape(KV_LORA, N_HEADS, QK_NOPE + V_DIM)
            mw = {
                "wdq": mla_w_dq[j].astype(BF16),
                "qn": _row(mla_q_norm[j]),
                "wuq": wuq.reshape(Q_LORA, N_HEADS * HEAD_PAD).astype(BF16),
                "wdkv": jnp.pad(mla_w_dkv[j], ((0, 0), (0, PE_PAD - QK_ROPE))).astype(BF16),
                "kvn": _row(mla_kv_norm[j]),
                "wk": wukv[:, :, :QK_NOPE].reshape(KV_LORA, N_HEADS * QK_NOPE).astype(BF16),
                "wvt": wukv[:, :, QK_NOPE:].reshape(KV_LORA, N_HEADS * V_DIM).T.astype(BF16),
                "qhn": _row(jnp.pad(mla_q_hnorm[j], (0, HEAD_PAD - QK_DIM))),
                "khn": _row(jnp.pad(mla_k_hnorm[j], (0, HEAD_PAD - QK_DIM))),
            }
            qp, kp, vtp, ckv_p, kpe_p = _mla_proj(xp, mod, cond_prompt, gmix, mw, tp, None)
            new_ckv.append(ckv_p)
            new_kpe.append(kpe_p)
            op = _attention(qp, kp, vtp, None, tp, N_HEADS, tp)
            cache = _cache_expand(cache_ckv, cache_kpe_pad, j, mw)
            qs, ks, vts = _mla_proj(xs, mod, cond_sample_batch, gmix, mw, 512, rope)
            os_ = _attention(qs, ks, vts, cache, 512, 4, 1024)
            xp2 = _ffn(xp.reshape(bp * tp, D_MODEL), op.reshape(bp * tp, D_MODEL), wo, j, mod,
                       cond_prompt, gffn, fw, l, ffn_tm)
            xs2 = _ffn(xs.reshape(bs * ts, D_MODEL), os_.reshape(bs * ts, D_MODEL), wo, j, mod,
                       cond_sample_blk, gffn, fw, l, ffn_tm)
        else:
            cw = {
                "w1": conv_w1, "b1": _row(conv_b_pw1[j]),
                "wdw": conv_w_dw[j].reshape(CONV_WIDTH, D_MODEL // 128, 128),
                "bdw": conv_b_dw[j].reshape(D_MODEL // 128, 128),
                "lng": _row(conv_ln_g[j]), "lnb": _row(conv_ln_b[j]),
                "w2": conv_w2, "b2": _row(conv_b_pw2[j]),
            }
            xp1 = _conv_module(xp, mod, cond_prompt, gmix, cw, j, tp)
            xs1 = _conv_module(xs, mod, cond_sample_batch, gmix, cw, j, 512)
            xp2 = _ffn(xp1.reshape(bp * tp, D_MODEL), None, None, None, mod, cond_prompt, gffn, fw,
                       l, ffn_tm)
            xs2 = _ffn(xs1.reshape(bs * ts, D_MODEL), None, None, None, mod, cond_sample_blk, gffn,
                       fw, l, ffn_tm)
        xp = xp2.reshape(bp, tp, D_MODEL)
        xs = xs2.reshape(bs, ts, D_MODEL)

    state_ckv = jnp.stack(new_ckv, axis=1)
    state_kpe = jnp.stack(new_kpe, axis=1)
    return (xp, xs, state_ckv, state_kpe)
```
